```python
import math, functools
import jax, jax.numpy as jnp
from jax import lax
import numpy as np

D_MODEL = 2048
BATCH = 2
SEQ = 8192
DEPTH = 4

MLA_HEADS = 8
MLA_Q_RANK = 768
MLA_KV_RANK = 512
MLA_NOPE = 128
MLA_ROPE = 64
MLA_V = 128
ROPE_THETA = 10000.0
ATTN_BLOCK = 128
GDN_HEADS = 8
GDN_DK = 128
GDN_DV = 128
GDN_CONV = 4
GDN_CHUNK = 64
RWKV_HEADS = 16
RWKV_HEAD = 64
RWKV_W_RANK = 64
RWKV_A_RANK = 64
RWKV_G_RANK = 128
RWKV_LN_EPS = 64e-5
D_FF = -(-8 * D_MODEL // (3 * 256)) * 256
NORM_EPS = 1e-6
MAX_POS_OFFSET = 4096

MLA_IN = MLA_Q_RANK + MLA_KV_RANK + MLA_ROPE
GDN_QKV = GDN_HEADS * (2 * GDN_DK + GDN_DV)
GDN_IN = GDN_QKV + GDN_HEADS * GDN_DV + 2 * GDN_HEADS
RWKV_WIDTH = RWKV_HEADS * RWKV_HEAD
RWKV_IN = 3 * RWKV_WIDTH + RWKV_W_RANK + RWKV_A_RANK + RWKV_G_RANK
GATE_IN = 3 * D_MODEL
IN_WIDTH = MLA_IN + GDN_IN + RWKV_IN + GATE_IN
MLA_OUT = MLA_HEADS * MLA_V
GDN_OUT = GDN_HEADS * GDN_DV
RWKV_OUT = RWKV_WIDTH
MIX_WIDTH = MLA_OUT + GDN_OUT + RWKV_OUT

kernel_name = 'hybrid_mla_gdn_rwkv7_adaln_block'


def rmsnorm(x, w, eps=NORM_EPS):
    xf = x.astype(jnp.float32)
    y = xf * lax.rsqrt(jnp.mean(xf * xf, axis=-1, keepdims=True) + eps)
    return (y * w.astype(jnp.float32)).astype(x.dtype)


def l2norm(x, eps=1e-6):
    xf = x.astype(jnp.float32)
    return xf * lax.rsqrt(jnp.sum(xf * xf, axis=-1, keepdims=True) + eps)


def modulate(x, norm_w, shift, scale):
    return rmsnorm(x, norm_w) * (1.0 + scale) + shift


def rope_tables(positions):
    inv = 1.0 / (ROPE_THETA ** (jnp.arange(0, MLA_ROPE, 2, dtype=jnp.float32) / MLA_ROPE))
    ang = positions.astype(jnp.float32)[..., None] * inv
    return jnp.cos(ang), jnp.sin(ang)


def apply_rope(x, cos, sin):
    x1, x2 = jnp.split(x.astype(jnp.float32), 2, axis=-1)
    return jnp.concatenate([x1 * cos - x2 * sin, x1 * sin + x2 * cos], axis=-1).astype(x.dtype)


def mla_mixer(p_in, cos, sin, q_norm_w, w_uq, kv_norm_w, w_ukv):
    B, S, _ = p_in.shape
    H = MLA_HEADS
    c_q, c_kv, k_r = jnp.split(p_in, [MLA_Q_RANK, MLA_Q_RANK + MLA_KV_RANK], axis=-1)
    q = (rmsnorm(c_q, q_norm_w) @ w_uq).reshape(B, S, H, MLA_NOPE + MLA_ROPE)
    q_nope = q[..., :MLA_NOPE]
    q_rope = apply_rope(q[..., MLA_NOPE:], cos[:, :, None, :], sin[:, :, None, :])
    kv = (rmsnorm(c_kv, kv_norm_w) @ w_ukv).reshape(B, S, H, MLA_NOPE + MLA_V)
    k_nope, v = kv[..., :MLA_NOPE], kv[..., MLA_NOPE:]
    k_rope = apply_rope(k_r, cos, sin)
    scale = (MLA_NOPE + MLA_ROPE) ** -0.5
    nb = S // ATTN_BLOCK
    qn_b = q_nope.reshape(B, nb, ATTN_BLOCK, H, MLA_NOPE).transpose(1, 0, 2, 3, 4)
    qr_b = q_rope.reshape(B, nb, ATTN_BLOCK, H, MLA_ROPE).transpose(1, 0, 2, 3, 4)
    k_idx = jnp.arange(S)

    def block(args):
        qn, qr, start = args
        s = (jnp.einsum('bqhd,bkhd->bhqk', qn, k_nope)
             + jnp.einsum('bqhr,bkr->bhqk', qr, k_rope)).astype(jnp.float32) * scale
        q_idx = start + jnp.arange(ATTN_BLOCK)
        s = jnp.where(q_idx[:, None] >= k_idx[None, :], s, -jnp.inf)
        pr = jax.nn.softmax(s, axis=-1).astype(v.dtype)
        return jnp.einsum('bhqk,bkhd->bqhd', pr, v)

    o = lax.map(block, (qn_b, qr_b, jnp.arange(nb) * ATTN_BLOCK))
    return o.transpose(1, 0, 2, 3, 4).reshape(B, S, H * MLA_V)


def causal_dwconv(x, w):
    return lax.conv_general_dilated(x, w[:, None, :], window_strides=(1,),
                                    padding=((GDN_CONV - 1, 0),),
                                    dimension_numbers=('NWC', 'WIO', 'NWC'),
                                    feature_group_count=x.shape[-1])


def gdn_mixer(p_in, conv_w, a_log, dt_bias, norm_w):
    B, S, _ = p_in.shape
    H, DK, DV, C = GDN_HEADS, GDN_DK, GDN_DV, GDN_CHUNK
    n = S // C
    f32 = jnp.float32
    qkv, z, b, a = jnp.split(p_in, [GDN_QKV, GDN_QKV + H * DV, GDN_QKV + H * DV + H], axis=-1)
    qkv = jax.nn.silu(causal_dwconv(qkv, conv_w))
    q, k, v = jnp.split(qkv, [H * DK, 2 * H * DK], axis=-1)
    q = l2norm(q.reshape(B, S, H, DK)) * (DK ** -0.5)
    k = l2norm(k.reshape(B, S, H, DK))
    v = v.reshape(B, S, H, DV).astype(f32)
    beta = jax.nn.sigmoid(b.astype(f32))
    g = -jnp.exp(a_log.astype(f32)) * jax.nn.softplus(a.astype(f32) + dt_bias.astype(f32))

    def to_chunks(t):
        return t.reshape(B, n, C, H, t.shape[-1]).transpose(0, 3, 1, 2, 4)

    q, k, v = to_chunks(q), to_chunks(k), to_chunks(v)
    beta = beta.reshape(B, n, C, H).transpose(0, 3, 1, 2)
    G = jnp.cumsum(g.reshape(B, n, C, H).transpose(0, 3, 1, 2), axis=-1)
    idx = jnp.arange(C)
    incl = idx[:, None] >= idx[None, :]
    strict = idx[:, None] > idx[None, :]
    decay = jnp.exp(jnp.where(incl, G[..., :, None] - G[..., None, :], -jnp.inf))
    kk = jnp.einsum('bhnid,bhnjd->bhnij', k, k)
    A = jnp.where(strict, kk * decay, 0.0) * beta[..., :, None]
    T = A + jnp.eye(C, dtype=f32)
    solve = functools.partial(lax.linalg.triangular_solve, left_side=True, lower=True, unit_diagonal=True)
    U = solve(T, beta[..., None] * v)
    Wk = solve(T, (beta * jnp.exp(G))[..., None] * k)
    QK = jnp.einsum('bhnid,bhnjd->bhnij', q, k) * decay
    qg = q * jnp.exp(G)[..., None]
    kg = k * jnp.exp(G[..., -1:] - G)[..., None]
    gC = jnp.exp(G[..., -1])

    def step(state, inp):
        u_c, wk_c, qk_c, qg_c, kg_c, gc_c = inp
        w_c = u_c - jnp.einsum('bhcd,bhde->bhce', wk_c, state)
        o_c = jnp.einsum('bhcd,bhde->bhce', qg_c, state) + jnp.einsum('bhcj,bhje->bhce', qk_c, w_c)
        state = gc_c[..., None, None] * state + jnp.einsum('bhcd,bhce->bhde', kg_c, w_c)
        return state, o_c

    xs = tuple(jnp.moveaxis(t, 2, 0) for t in (U, Wk, QK, qg, kg, gC))
    _, o = lax.scan(step, jnp.zeros((B, H, DK, DV), f32), xs)
    o = o.transpose(1, 0, 3, 2, 4).reshape(B, S, H, DV)
    o = rmsnorm(o, norm_w) * jax.nn.silu(z.reshape(B, S, H, DV).astype(f32))
    return o.reshape(B, S, H * DV).astype(p_in.dtype)


def rwkv_mixer(p_in, mu, w0, w_up, a0, a_up, g_up, k_k, k_a, r_k, lnx_w, lnx_b):
    B, S, _ = p_in.shape
    H, N, W = RWKV_HEADS, RWKV_HEAD, RWKV_WIDTH
    f32 = jnp.float32
    prev = jnp.concatenate([jnp.zeros_like(p_in[:, :1]), p_in[:, :-1]], axis=1)
    xs = p_in + (prev - p_in) * mu
    r, k, v, xw, xa, xg = jnp.split(
        xs, [W, 2 * W, 3 * W, 3 * W + RWKV_W_RANK, 3 * W + RWKV_W_RANK + RWKV_A_RANK], axis=-1)
    w_log = -jax.nn.softplus(-(w0 + jnp.tanh(xw) @ w_up).astype(f32)) - 0.5
    decay = jnp.exp(-jnp.exp(w_log))
    a = jax.nn.sigmoid((a0 + xa @ a_up).astype(f32))
    g = (jax.nn.sigmoid(xg) @ g_up).astype(f32)
    r, k, v = r.astype(f32), k.astype(f32), v.astype(f32)
    kk = l2norm((k * k_k.astype(f32)).reshape(B, S, H, N))
    k = k * (1.0 + (a - 1.0) * k_a.astype(f32))

    def heads(t):
        return t.reshape(B, S, H, N)

    r, k, v, decay, a = heads(r), heads(k), heads(v), heads(decay), heads(a)
    b_vec = kk * a

    def step(state, inp):
        r_t, w_t, k_t, v_t, kk_t, b_t = inp
        sa = jnp.einsum('bhvk,bhk->bhv', state, kk_t)
        state = (state * w_t[:, :, None, :] - sa[..., None] * b_t[:, :, None, :]
                 + v_t[..., None] * k_t[:, :, None, :])
        return state, jnp.einsum('bhvk,bhk->bhv', state, r_t)

    def tm(t):
        return jnp.moveaxis(t, 1, 0)

    _, y = lax.scan(step, jnp.zeros((B, H, N, N), f32),
                    (tm(r), tm(decay), tm(k), tm(v), tm(kk), tm(b_vec)))
    y = jnp.moveaxis(y, 0, 1)
    mean = jnp.mean(y, axis=-1, keepdims=True)
    var = jnp.mean(jnp.square(y - mean), axis=-1, keepdims=True)
    yn = ((y - mean) * lax.rsqrt(var + RWKV_LN_EPS)).reshape(B, S, W)
    yn = yn * lnx_w.astype(f32) + lnx_b.astype(f32)
    bonus = (jnp.sum(r * k * r_k.astype(f32), axis=-1, keepdims=True) * v).reshape(B, S, W)
    return ((yn + bonus) * g).astype(p_in.dtype)


def setup_inputs(seed: int = 0) -> dict:
    key = jax.random.key(seed)
    ks = iter(jax.random.split(key, 40))
    f32 = jnp.float32
    L, D = DEPTH, D_MODEL

    def nrm(shape, scale):
        return jax.random.normal(next(ks), shape, f32) * scale

    def gain(shape):
        return 1.0 + nrm(shape, 0.02)

    def unif(shape, lo, hi):
        return jax.random.uniform(next(ks), shape, f32, lo, hi)

    x = nrm((BATCH, SEQ, D), 1.0)
    c = nrm((BATCH, D), 1.0)
    offset = jax.random.randint(next(ks), (BATCH, 1), 0, MAX_POS_OFFSET, dtype=jnp.int32)
    positions = offset + jnp.arange(SEQ, dtype=jnp.int32)[None, :]
    dt = jnp.exp(unif((L, GDN_HEADS), math.log(1e-3), math.log(1e-1)))
    return {
        'x': x,
        'c': c,
        'positions': positions,
        'w_ada': nrm((L, D, 6 * D), 0.5 * D ** -0.5),
        'b_ada': nrm((L, 6 * D), 0.02),
        'norm1_w': gain((L, D)),
        'w_in': nrm((L, D, IN_WIDTH), D ** -0.5),
        'mla_q_norm_w': gain((L, MLA_Q_RANK)),
        'mla_w_uq': nrm((L, MLA_Q_RANK, MLA_HEADS * (MLA_NOPE + MLA_ROPE)), MLA_Q_RANK ** -0.5),
        'mla_kv_norm_w': gain((L, MLA_KV_RANK)),
        'mla_w_ukv': nrm((L, MLA_KV_RANK, MLA_HEADS * (MLA_NOPE + MLA_V)), MLA_KV_RANK ** -0.5),
        'gdn_conv_w': nrm((L, GDN_CONV, GDN_QKV), GDN_CONV ** -0.5),
        'gdn_a_log': jnp.log(unif((L, GDN_HEADS), 1.0, 16.0)),
        'gdn_dt_bias': dt + jnp.log(-jnp.expm1(-dt)),
        'gdn_norm_w': gain((L, GDN_DV)),
        'rwkv_mu': unif((L, RWKV_IN), 0.0, 1.0),
        'rwkv_w0': unif((L, RWKV_WIDTH), -6.0, -1.0),
        'rwkv_w_up': nrm((L, RWKV_W_RANK, RWKV_WIDTH), 0.5 * RWKV_W_RANK ** -0.5),
        'rwkv_a0': nrm((L, RWKV_WIDTH), 0.1),
        'rwkv_a_up': nrm((L, RWKV_A_RANK, RWKV_WIDTH), RWKV_A_RANK ** -0.5),
        'rwkv_g_up': nrm((L, RWKV_G_RANK, RWKV_WIDTH), RWKV_G_RANK ** -0.5),
        'rwkv_k_k': 0.85 + nrm((L, RWKV_WIDTH), 0.02),
        'rwkv_k_a': gain((L, RWKV_WIDTH)),
        'rwkv_r_k': nrm((L, RWKV_HEADS, RWKV_HEAD), 0.1),
        'rwkv_lnx_w': gain((L, RWKV_WIDTH)),
        'rwkv_lnx_b': nrm((L, RWKV_WIDTH), 0.02),
        'w_branch': nrm((L, MIX_WIDTH, D), (MIX_WIDTH // 3) ** -0.5),
        'w_out': nrm((L, D, D), D ** -0.5),
        'norm2_w': gain((L, D)),
        'w_gate_up': nrm((L, D, 2 * D_FF), D ** -0.5),
        'w_down': nrm((L, D_FF, D), D_FF ** -0.5),
        'final_norm_w': gain((D,)),
    }


def reference(x, c, positions, w_ada, b_ada, norm1_w, w_in, mla_q_norm_w, mla_w_uq, mla_kv_norm_w,
              mla_w_ukv, gdn_conv_w, gdn_a_log, gdn_dt_bias, gdn_norm_w, rwkv_mu, rwkv_w0, rwkv_w_up,
              rwkv_a0, rwkv_a_up, rwkv_g_up, rwkv_k_k, rwkv_k_a, rwkv_r_k, rwkv_lnx_w, rwkv_lnx_b,
              w_branch, w_out, norm2_w, w_gate_up, w_down, final_norm_w):
    cos, sin = rope_tables(positions)
    c_act = jax.nn.silu(c)
    for l in range(DEPTH):
        mod = c_act @ w_ada[l] + b_ada[l]
        sh1, sc1, gt1, sh2, sc2, gt2 = jnp.split(mod[:, None, :], 6, axis=-1)
        h = modulate(x, norm1_w[l], sh1, sc1)
        p = h @ w_in[l]
        p_a, p_b, p_c, p_g = jnp.split(p, [MLA_IN, MLA_IN + GDN_IN, MLA_IN + GDN_IN + RWKV_IN], axis=-1)
        o_a = mla_mixer(p_a, cos, sin, mla_q_norm_w[l], mla_w_uq[l], mla_kv_norm_w[l], mla_w_ukv[l])
        o_b = gdn_mixer(p_b, gdn_conv_w[l], gdn_a_log[l], gdn_dt_bias[l], gdn_norm_w[l])
        o_c = rwkv_mixer(p_c, rwkv_mu[l], rwkv_w0[l], rwkv_w_up[l], rwkv_a0[l], rwkv_a_up[l],
                         rwkv_g_up[l], rwkv_k_k[l], rwkv_k_a[l], rwkv_r_k[l], rwkv_lnx_w[l], rwkv_lnx_b[l])
        wb_a, wb_b, wb_c = jnp.split(w_branch[l], [MLA_OUT, MLA_OUT + GDN_OUT], axis=0)
        g_a, g_b, g_c = jnp.split(jax.nn.sigmoid(p_g), 3, axis=-1)
        merged = g_a * (o_a @ wb_a) + g_b * (o_b @ wb_b) + g_c * (o_c @ wb_c)
        x = x + gt1 * (merged @ w_out[l])
        h2 = modulate(x, norm2_w[l], sh2, sc2)
        gate, up = jnp.split(h2 @ w_gate_up[l], 2, axis=-1)
        x = x + gt2 * ((jax.nn.silu(gate) * up) @ w_down[l])
    return rmsnorm(x, final_norm_w)
```

```python
import functools
import math

import jax
import jax.numpy as jnp
from jax import lax
from jax.experimental import pallas as pl
from jax.experimental.pallas import tpu as pltpu

F32 = jnp.float32
BF16 = jnp.bfloat16

D_MODEL = 2048
MLA_HEADS = 8
MLA_Q_RANK = 768
MLA_KV_RANK = 512
MLA_NOPE = 128
MLA_ROPE = 64
MLA_V = 128
ROPE_THETA = 10000.0
GDN_HEADS = 8
GDN_DK = 128
GDN_DV = 128
GDN_CONV = 4
RWKV_HEADS = 16
RWKV_HEAD = 64
RWKV_WIDTH = RWKV_HEADS * RWKV_HEAD
RWKV_W_RANK = 64
RWKV_A_RANK = 64
RWKV_G_RANK = 128
RWKV_LN_EPS = 64e-5
D_FF = 5632
NORM_EPS = 1e-6
CHUNK = 64
LANES = 128
SUBLANES = 8
VMEM_LIMIT = 56 * 1024 * 1024

CQ_OFF = 0
KR_OFF = 768
CKV_OFF = 1024
GQ_OFF = 2048
RR_OFF = 6144
GATE_OFF = 9216
NP1 = 15360
NP2 = 384

_O_CQ = 0
_O_CKV = 768
_O_KR = 1280
_O_GDN = 1344
_O_GZ = _O_GDN + 3072
_O_GB = _O_GZ + 1024
_O_GA = _O_GB + 8
_O_RWKV = _O_GA + 8
_O_XW = _O_RWKV + 3072
_O_XA = _O_XW + 64
_O_XG = _O_XA + 64
_O_GATE = _O_XG + 128


def _cparams(*sem):
    return pltpu.CompilerParams(dimension_semantics=sem, vmem_limit_bytes=VMEM_LIMIT)


def _dot(a, b):
    return jnp.dot(a.astype(BF16), b.astype(BF16), preferred_element_type=F32)


def _dot_nt(a, b):
    return lax.dot_general(a.astype(BF16), b.astype(BF16), (((1,), (1,)), ((), ())),
                           preferred_element_type=F32)


def _dot_tn(a, b):
    return lax.dot_general(a.astype(BF16), b.astype(BF16), (((0,), (0,)), ((), ())),
                           preferred_element_type=F32)


def _split(a, terms):
    parts = []
    rem = a
    for _ in range(terms):
        p = rem.astype(BF16)
        parts.append(p)
        rem = rem - p.astype(F32)
    return parts


def _dot3(a, b):
    ah, al = _split(a, 2)
    bh, bl = _split(b, 2)
    return _dot(ah, bh) + (_dot(ah, bl) + _dot(al, bh))


def _dot_exact_lhs(a_bf, b, terms):
    out = None
    for p in _split(b, terms):
        t = _dot(a_bf, p)
        out = t if out is None else out + t
    return out


def _dot_exact_rhs(a, b_bf, terms):
    out = None
    for p in _split(a, terms):
        t = _dot(p, b_bf)
        out = t if out is None else out + t
    return out


def _iota(shape, dim):
    return lax.broadcasted_iota(jnp.int32, shape, dim)


def _tri_inv(a, n):
    eye = (_iota((n, n), 0) == _iota((n, n), 1)).astype(F32)
    x = eye - a
    p = _dot3(a, a)
    k = 2
    while True:
        x = x + _dot3(x, p)
        k *= 2
        if k >= n:
            break
        p = _dot3(p, p)
    return x


def _sigmoid(x):
    return jax.nn.sigmoid(x)


def _softplus(x):
    return jnp.maximum(x, 0.0) + jnp.log1p(jnp.exp(-jnp.abs(x)))


def _mod_kernel(c_ref, w_ref, b_ref, o_ref):
    w = w_ref[...]
    for m in range(c_ref.shape[0]):
        c = c_ref[m]
        ca = c * _sigmoid(c)
        o_ref[m:m + 1, :] = jnp.sum(ca * w, axis=0, keepdims=True) + b_ref[...]


def adaln_mod(c, w_ada, b_ada):
    nl, d, n = w_ada.shape
    b = c.shape[0]
    tn = min(1024, n)
    return pl.pallas_call(
        _mod_kernel,
        grid=(nl, n // tn),
        in_specs=[pl.BlockSpec((b, d, 1), lambda l, j: (0, 0, 0)),
                  pl.BlockSpec((None, d, tn), lambda l, j: (l, 0, j)),
                  pl.BlockSpec((None, 1, tn), lambda l, j: (l, 0, j))],
        out_specs=pl.BlockSpec((None, b, tn), lambda l, j: (l, 0, j)),
        out_shape=jax.ShapeDtypeStruct((nl, b, n), F32),
        compiler_params=_cparams("arbitrary", "arbitrary"),
        name="adaln_mod",
    )(c[:, :, None], w_ada, b_ada[:, None, :])


def _modulate_to(h_scr, x_ref, nw_ref, sh_ref, sc_ref):
    x = x_ref[...]
    ms = jnp.mean(x * x, axis=-1, keepdims=True)
    y = x * lax.rsqrt(ms + NORM_EPS) * nw_ref[...]
    h_scr[...] = (y * (1.0 + sc_ref[...]) + sh_ref[...]).astype(BF16)


def _norm_gemm_kernel(x_ref, nw_ref, sh_ref, sc_ref, w_ref, o_ref, h_scr):
    @pl.when(pl.program_id(1) == 0)
    def _():
        _modulate_to(h_scr, x_ref, nw_ref, sh_ref, sc_ref)

    o_ref[...] = jnp.dot(h_scr[...], w_ref[...], preferred_element_type=F32).astype(o_ref.dtype)


def _mod_spec(d, layer, nb, which, tiles_per_seq):
    return pl.BlockSpec((None, 1, d),
                        lambda i, j: ((layer * nb + i // tiles_per_seq) * 6 + which, 0, 0))


def norm_gemm(x, norm_w, modr, w, layer, which_sh, nb, seq, out_dtype, tm, tn):
    t, d = x.shape
    n = w.shape[-1]
    tm = min(tm, seq)
    tn = min(tn, n)
    tps = seq // tm
    return pl.pallas_call(
        _norm_gemm_kernel,
        grid=(t // tm, n // tn),
        in_specs=[pl.BlockSpec((tm, d), lambda i, j: (i, 0)),
                  pl.BlockSpec((None, 1, d), lambda i, j: (layer, 0, 0)),
                  _mod_spec(d, layer, nb, which_sh, tps),
                  _mod_spec(d, layer, nb, which_sh + 1, tps),
                  pl.BlockSpec((None, d, tn), lambda i, j: (layer, 0, j))],
        out_specs=pl.BlockSpec((tm, tn), lambda i, j: (i, j)),
        out_shape=jax.ShapeDtypeStruct((t, n), out_dtype),
        scratch_shapes=[pltpu.VMEM((tm, d), BF16)],
        compiler_params=_cparams("arbitrary", "arbitrary"),
        name="norm_gemm",
    )(x, norm_w, modr, modr, w)


def _ffn_up_kernel(x_ref, nw_ref, sh_ref, sc_ref, wg_ref, wu_ref, o_ref, h_scr):
    @pl.when(pl.program_id(1) == 0)
    def _():
        _modulate_to(h_scr, x_ref, nw_ref, sh_ref, sc_ref)

    h = h_scr[...]
    gate = jnp.dot(h, wg_ref[...], preferred_element_type=F32)
    up = jnp.dot(h, wu_ref[...], preferred_element_type=F32)
    o_ref[...] = (gate * _sigmoid(gate) * up).astype(o_ref.dtype)


def ffn_up(x, norm_w, modr, w_gu, layer, nb, seq, tm, tn):
    t, d = x.shape
    f = w_gu.shape[-1] // 2
    tm = min(tm, seq)
    tn = min(tn, f)
    tps = seq // tm
    nf = f // tn
    return pl.pallas_call(
        _ffn_up_kernel,
        grid=(t // tm, nf),
        in_specs=[pl.BlockSpec((tm, d), lambda i, j: (i, 0)),
                  pl.BlockSpec((None, 1, d), lambda i, j: (layer, 0, 0)),
                  _mod_spec(d, layer, nb, 3, tps),
                  _mod_spec(d, layer, nb, 4, tps),
                  pl.BlockSpec((None, d, tn), lambda i, j: (layer, 0, j)),
                  pl.BlockSpec((None, d, tn), lambda i, j: (layer, 0, j + nf))],
        out_specs=pl.BlockSpec((tm, tn), lambda i, j: (i, j)),
        out_shape=jax.ShapeDtypeStruct((t, f), BF16),
        scratch_shapes=[pltpu.VMEM((tm, d), BF16)],
        compiler_params=_cparams("arbitrary", "arbitrary"),
        name="ffn_up",
    )(x, norm_w, modr, modr, w_gu, w_gu)


def _resid_gemm_kernel(a_ref, w_ref, x_ref, gt_ref, o_ref):
    y = jnp.dot(a_ref[...], w_ref[...], preferred_element_type=F32)
    o_ref[...] = x_ref[...] + gt_ref[...] * y


def resid_gemm(a, w, x, modr, layer, which_gt, nb, seq, tm, tn):
    t, k = a.shape
    d = x.shape[-1]
    tm = min(tm, seq)
    tn = min(tn, d)
    tps = seq // tm
    return pl.pallas_call(
        _resid_gemm_kernel,
        grid=(t // tm, d // tn),
        in_specs=[pl.BlockSpec((tm, k), lambda i, j: (i, 0)),
                  pl.BlockSpec((None, k, tn), lambda i, j: (layer, 0, j)),
                  pl.BlockSpec((tm, tn), lambda i, j: (i, j)),
                  pl.BlockSpec((None, 1, tn),
                               lambda i, j: ((layer * nb + i // tps) * 6 + which_gt, 0, j))],
        out_specs=pl.BlockSpec((tm, tn), lambda i, j: (i, j)),
        out_shape=jax.ShapeDtypeStruct((t, d), F32),
        compiler_params=_cparams("arbitrary", "arbitrary"),
        name="resid_gemm",
    )(a, w, x, modr)


def _merge_kernel(oa_ref, ob_ref, oc_ref, wa_ref, wb_ref, wc_ref, ga_ref, gb_ref, gc_ref, o_ref):
    def branch(o_r, w_r, g_r):
        y = jnp.dot(o_r[...], w_r[...], preferred_element_type=F32)
        return _sigmoid(g_r[...].astype(F32)) * y

    acc = branch(oa_ref, wa_ref, ga_ref) + branch(ob_ref, wb_ref, gb_ref) + branch(oc_ref, wc_ref, gc_ref)
    o_ref[...] = acc.astype(o_ref.dtype)


def merge_gemm(o_a, o_b, o_c, w_branch, p1, layer, tm, tn):
    t, kb = o_a.shape
    d = w_branch.shape[-1]
    tm = min(tm, t)
    tn = min(tn, d)
    gblk = GATE_OFF // tn
    dblk = d // tn

    def ospec():
        return pl.BlockSpec((tm, kb), lambda i, j: (i, 0))

    def wspec(r):
        return pl.BlockSpec((None, kb, tn), lambda i, j: (layer, r, j))

    def gspec(r):
        return pl.BlockSpec((tm, tn), lambda i, j: (i, gblk + r * dblk + j))

    return pl.pallas_call(
        _merge_kernel,
        grid=(t // tm, d // tn),
        in_specs=[ospec(), ospec(), ospec(), wspec(0), wspec(1), wspec(2), gspec(0), gspec(1), gspec(2)],
        out_specs=pl.BlockSpec((tm, tn), lambda i, j: (i, j)),
        out_shape=jax.ShapeDtypeStruct((t, d), BF16),
        compiler_params=_cparams("arbitrary", "arbitrary"),
        name="merge_gemm",
    )(o_a, o_b, o_c, w_branch, w_branch, w_branch, p1, p1, p1)


def _final_norm_kernel(x_ref, w_ref, o_ref):
    x = x_ref[...]
    ms = jnp.mean(x * x, axis=-1, keepdims=True)
    o_ref[...] = x * lax.rsqrt(ms + NORM_EPS) * w_ref[...]


def final_norm(x, w, tm):
    t, d = x.shape
    tm = min(tm, t)
    return pl.pallas_call(
        _final_norm_kernel,
        grid=(t // tm,),
        in_specs=[pl.BlockSpec((tm, d), lambda i: (i, 0)), pl.BlockSpec((1, d), lambda i: (0, 0))],
        out_specs=pl.BlockSpec((tm, d), lambda i: (i, 0)),
        out_shape=jax.ShapeDtypeStruct((t, d), F32),
        compiler_params=_cparams("arbitrary"),
        name="final_norm",
    )(x, w[None, :])


def _rms_rows(x, w):
    ms = jnp.mean(x * x, axis=-1, keepdims=True)
    return x * lax.rsqrt(ms + NORM_EPS) * w


def _mla_proj_kernel(cq_ref, kr_ref, ckv_ref, cs_ref, qnw_ref, kvnw_ref, wq_ref, wkv_ref,
                     q_ref, k_ref, v_ref):
    scale = (MLA_NOPE + MLA_ROPE) ** -0.5
    cc = cs_ref[:, 0:LANES]
    ss = cs_ref[:, LANES:2 * LANES]
    nq = _rms_rows(cq_ref[...].astype(F32), qnw_ref[...]).astype(BF16)
    q3 = jnp.dot(nq, wq_ref[...], preferred_element_type=F32)
    for h in range(MLA_HEADS):
        b0 = h * 3 * LANES
        nope = q3[:, b0:b0 + LANES]
        rope = q3[:, b0 + LANES:b0 + 2 * LANES] * cc + q3[:, b0 + 2 * LANES:b0 + 3 * LANES] * ss
        q_ref[:, 2 * h * LANES:(2 * h + 1) * LANES] = (nope * scale).astype(BF16)
        q_ref[:, (2 * h + 1) * LANES:(2 * h + 2) * LANES] = (rope * scale).astype(BF16)
    nkv = _rms_rows(ckv_ref[...].astype(F32), kvnw_ref[...]).astype(BF16)
    kv = jnp.dot(nkv, wkv_ref[...], preferred_element_type=F32)
    kr = kr_ref[...].astype(F32)
    krope = (kr[:, 0:LANES] * cc + kr[:, LANES:2 * LANES] * ss).astype(BF16)
    for h in range(MLA_HEADS):
        k_ref[:, 2 * h * LANES:(2 * h + 1) * LANES] = kv[:, h * LANES:(h + 1) * LANES].astype(BF16)
        k_ref[:, (2 * h + 1) * LANES:(2 * h + 2) * LANES] = krope
    v_ref[...] = kv[:, MLA_HEADS * LANES:].astype(BF16)


def mla_proj(p1, cs, q_norm_w, kv_norm_w, wq3, wkv, layer, tm):
    t = p1.shape[0]
    tm = min(tm, t)
    hq = MLA_HEADS * 2 * LANES
    return pl.pallas_call(
        _mla_proj_kernel,
        grid=(t // tm,),
        in_specs=[pl.BlockSpec((tm, MLA_Q_RANK), lambda i: (i, CQ_OFF // MLA_Q_RANK)),
                  pl.BlockSpec((tm, 2 * LANES), lambda i: (i, KR_OFF // (2 * LANES))),
                  pl.BlockSpec((tm, MLA_KV_RANK), lambda i: (i, CKV_OFF // MLA_KV_RANK)),
                  pl.BlockSpec((tm, 2 * LANES), lambda i: (i, 0)),
                  pl.BlockSpec((None, 1, MLA_Q_RANK), lambda i: (layer, 0, 0)),
                  pl.BlockSpec((None, 1, MLA_KV_RANK), lambda i: (layer, 0, 0)),
                  pl.BlockSpec((None, MLA_Q_RANK, 3 * MLA_HEADS * LANES), lambda i: (layer, 0, 0)),
                  pl.BlockSpec((None, MLA_KV_RANK, 2 * MLA_HEADS * LANES), lambda i: (layer, 0, 0))],
        out_specs=[pl.BlockSpec((tm, hq), lambda i: (i, 0)),
                   pl.BlockSpec((tm, hq), lambda i: (i, 0)),
                   pl.BlockSpec((tm, MLA_HEADS * LANES), lambda i: (i, 0))],
        out_shape=[jax.ShapeDtypeStruct((t, hq), BF16),
                   jax.ShapeDtypeStruct((t, hq), BF16),
                   jax.ShapeDtypeStruct((t, MLA_HEADS * LANES), BF16)],
        compiler_params=_cparams("arbitrary"),
        name="mla_proj",
    )(p1, p1, p1, cs, q_norm_w, kv_norm_w, wq3, wkv)


def _flash_kernel(it_ref, jt_ref, q_ref, k_ref, v_ref, o_ref, m_scr, l_scr, acc_scr):
    t = pl.program_id(2)
    i = it_ref[t]
    j = jt_ref[t]

    @pl.when(j == 0)
    def _():
        m_scr[...] = jnp.full(m_scr.shape, -jnp.inf, F32)
        l_scr[...] = jnp.zeros(l_scr.shape, F32)
        acc_scr[...] = jnp.zeros(acc_scr.shape, F32)

    def step(masked):
        s = lax.dot_general(q_ref[...], k_ref[...], (((1,), (1,)), ((), ())),
                            preferred_element_type=F32)
        if masked:
            s = jnp.where(_iota(s.shape, 0) >= _iota(s.shape, 1), s, -jnp.inf)
        m_prev = m_scr[...]
        m_new = jnp.maximum(m_prev, jnp.max(s, axis=-1, keepdims=True))
        p = jnp.exp(s - m_new)
        alpha = jnp.exp(m_prev - m_new)
        l_scr[...] = alpha * l_scr[...] + jnp.sum(p, axis=-1, keepdims=True)
        acc_scr[...] = alpha * acc_scr[...] + jnp.dot(p.astype(BF16), v_ref[...],
                                                      preferred_element_type=F32)
        m_scr[...] = m_new

    @pl.when(j < i)
    def _():
        step(False)

    @pl.when(j == i)
    def _():
        step(True)
        o_ref[...] = (acc_scr[...] / l_scr[...]).astype(o_ref.dtype)


def flash_attn(q, k, v, nb, seq, tq):
    tq = min(tq, seq)
    nq = seq // tq
    pairs = [(i, j) for i in range(nq) for j in range(i + 1)]
    it = jnp.asarray([p[0] for p in pairs], jnp.int32)
    jt = jnp.asarray([p[1] for p in pairs], jnp.int32)
    grid_spec = pltpu.PrefetchScalarGridSpec(
        num_scalar_prefetch=2,
        grid=(nb, MLA_HEADS, len(pairs)),
        in_specs=[pl.BlockSpec((tq, 2 * LANES), lambda b, h, t, it_r, jt_r: (b * nq + it_r[t], h)),
                  pl.BlockSpec((tq, 2 * LANES), lambda b, h, t, it_r, jt_r: (b * nq + jt_r[t], h)),
                  pl.BlockSpec((tq, LANES), lambda b, h, t, it_r, jt_r: (b * nq + jt_r[t], h))],
        out_specs=pl.BlockSpec((tq, LANES), lambda b, h, t, it_r, jt_r: (b * nq + it_r[t], h)),
        scratch_shapes=[pltpu.VMEM((tq, 1), F32), pltpu.VMEM((tq, 1), F32), pltpu.VMEM((tq, LANES), F32)],
    )
    return pl.pallas_call(
        _flash_kernel,
        grid_spec=grid_spec,
        out_shape=jax.ShapeDtypeStruct((nb * seq, MLA_HEADS * LANES), BF16),
        compiler_params=_cparams("arbitrary", "arbitrary", "arbitrary"),
        name="flash_attn",
    )(it, jt, q, k, v)


def _shifted_rows(x, halo, d):
    xr = pltpu.roll(x, d, 0)
    hr = pltpu.roll(halo, d, 0)
    top = jnp.where(_iota(halo.shape, 0) < d, hr, xr[0:SUBLANES])
    return jnp.concatenate([top, xr[SUBLANES:]], axis=0)


def _gdn_prep_kernel(q_ref, k_ref, v_ref, qh_ref, kh_ref, vh_ref, cwq_ref, cwk_ref, cwv_ref,
                     p2_ref, alog_ref, dtb_ref, qo_ref, ko_ref, vo_ref, bg_ref, *, tiles_per_seq):
    first = (pl.program_id(0) % tiles_per_seq) == 0

    def conv_silu(x_ref, h_ref, w_ref):
        x = x_ref[...].astype(F32)
        halo = jnp.where(first, 0.0, h_ref[...].astype(F32))
        w = w_ref[...]
        acc = x * w[GDN_CONV - 1:GDN_CONV, :]
        for d in range(1, GDN_CONV):
            acc = acc + _shifted_rows(x, halo, d) * w[GDN_CONV - 1 - d:GDN_CONV - d, :]
        return acc * _sigmoid(acc)

    def l2n(x, mult):
        outs = []
        for h in range(GDN_HEADS):
            xh = x[:, h * LANES:(h + 1) * LANES]
            ss = jnp.sum(xh * xh, axis=-1, keepdims=True)
            outs.append(xh * (lax.rsqrt(ss + 1e-6) * mult))
        return jnp.concatenate(outs, axis=1)

    qo_ref[...] = l2n(conv_silu(q_ref, qh_ref, cwq_ref), GDN_DK ** -0.5).astype(BF16)
    ko_ref[...] = l2n(conv_silu(k_ref, kh_ref, cwk_ref), 1.0).astype(BF16)
    vo_ref[...] = conv_silu(v_ref, vh_ref, cwv_ref).astype(BF16)
    p2 = p2_ref[...]
    lane = _iota(p2.shape, 1)
    g = -jnp.exp(alog_ref[...]) * _softplus(p2 + dtb_ref[...])
    bg_ref[...] = jnp.where(lane < GDN_HEADS, _sigmoid(p2), g)


def gdn_prep(p1, p2, conv_w, a_log, dt_bias, layer, seq, tm):
    t = p1.shape[0]
    tm = min(tm, seq)
    w = GDN_HEADS * GDN_DK
    cblk = GQ_OFF // w
    hb = tm // SUBLANES

    def xspec(r):
        return pl.BlockSpec((tm, w), lambda i: (i, cblk + r))

    def hspec(r):
        return pl.BlockSpec((SUBLANES, w), lambda i: (jnp.maximum(i * hb - 1, 0), cblk + r))

    def cwspec(r):
        return pl.BlockSpec((None, GDN_CONV, w), lambda i: (layer, 0, r))

    pad = jnp.zeros((LANES - 2 * GDN_HEADS,), F32)
    z8 = jnp.zeros((GDN_HEADS,), F32)
    alog_row = jnp.concatenate([z8, a_log[layer], pad])[None, :]
    dtb_row = jnp.concatenate([z8, dt_bias[layer], pad])[None, :]
    return pl.pallas_call(
        functools.partial(_gdn_prep_kernel, tiles_per_seq=seq // tm),
        grid=(t // tm,),
        in_specs=[xspec(0), xspec(1), xspec(2), hspec(0), hspec(1), hspec(2),
                  cwspec(0), cwspec(1), cwspec(2),
                  pl.BlockSpec((tm, LANES), lambda i: (i, 2)),
                  pl.BlockSpec((1, LANES), lambda i: (0, 0)),
                  pl.BlockSpec((1, LANES), lambda i: (0, 0))],
        out_specs=[pl.BlockSpec((tm, w), lambda i: (i, 0))] * 3 + [pl.BlockSpec((tm, LANES), lambda i: (i, 0))],
        out_shape=[jax.ShapeDtypeStruct((t, w), BF16)] * 3 + [jax.ShapeDtypeStruct((t, LANES), F32)],
        compiler_params=_cparams("arbitrary"),
        name="gdn_prep",
    )(p1, p1, p1, p1, p1, p1, conv_w, conv_w, conv_w, p2, alog_row, dtb_row)


def _gdn_chunk_kernel(q_ref, k_ref, v_ref, z_ref, bg_ref, nw_ref, o_ref, st_ref):
    c = CHUNK
    nchunk = q_ref.shape[0] // c

    @pl.when(pl.program_id(1) == 0)
    def _():
        st_ref[...] = jnp.zeros(st_ref.shape, F32)

    rows_n = q_ref.shape[0]
    bg = bg_ref[...]
    r = _iota((rows_n, rows_n), 0)
    cidx = _iota((rows_n, rows_n), 1)
    lmat = jnp.where(((r // c) == (cidx // c)) & (r >= cidx), 1.0, 0.0).astype(BF16)
    g_cum = _dot_exact_lhs(lmat, bg, 3)
    g_cum_t = g_cum.T
    i64 = _iota((c, c), 0)
    j64 = _iota((c, c), 1)
    incl = i64 >= j64
    strict = i64 > j64
    nw = nw_ref[...]
    states = [st_ref[h] for h in range(GDN_HEADS)]
    for ci in range(nchunk):
        rows = slice(ci * c, (ci + 1) * c)
        for h in range(GDN_HEADS):
            cols = slice(h * LANES, (h + 1) * LANES)
            q = q_ref[rows, cols]
            k = k_ref[rows, cols]
            kf = k.astype(F32)
            vf = v_ref[rows, cols].astype(F32)
            gc = g_cum[rows, GDN_HEADS + h:GDN_HEADS + h + 1]
            gr = g_cum_t[GDN_HEADS + h:GDN_HEADS + h + 1, rows]
            beta = bg[rows, h:h + 1]
            dec = jnp.exp(jnp.where(incl, gc - gr, -jnp.inf))
            a = jnp.where(strict, _dot_nt(k, k) * dec, 0.0) * beta
            tinv = _tri_inv(a, c)
            eg = jnp.exp(gc)
            uw = _dot3(tinv, jnp.concatenate([beta * vf, (beta * eg) * kf], axis=1))
            u = uw[:, :GDN_DV]
            wk = uw[:, GDN_DV:]
            qk = _dot_nt(q, k) * dec
            qg = q.astype(F32) * eg
            g_last = gc[c - 1:c, :]
            kg = kf * jnp.exp(g_last - gc)
            s = states[h]
            w_c = u - _dot(wk, s)
            o_c = _dot(qg, s) + _dot(qk, w_c)
            states[h] = jnp.exp(g_last) * s + _dot_tn(kg, w_c)
            zf = z_ref[rows, cols].astype(F32)
            on = _rms_rows(o_c, nw)
            o_ref[rows, cols] = (on * (zf * _sigmoid(zf))).astype(o_ref.dtype)
    for h in range(GDN_HEADS):
        st_ref[h] = states[h]


def gdn_chunk(qn, kn, vc, p1, bg, norm_w, layer, nb, seq, tm):
    t, w = qn.shape
    tm = min(tm, seq)
    spt = seq // tm
    zblk = (GQ_OFF + 3 * w) // w

    def xspec(cb):
        return pl.BlockSpec((tm, w), lambda b, s: (b * spt + s, cb))

    return pl.pallas_call(
        _gdn_chunk_kernel,
        grid=(nb, spt),
        in_specs=[xspec(0), xspec(0), xspec(0), xspec(zblk),
                  pl.BlockSpec((tm, LANES), lambda b, s: (b * spt + s, 0)),
                  pl.BlockSpec((None, 1, GDN_DV), lambda b, s: (layer, 0, 0))],
        out_specs=xspec(0),
        out_shape=jax.ShapeDtypeStruct((t, w), BF16),
        scratch_shapes=[pltpu.VMEM((GDN_HEADS, GDN_DK, GDN_DV), F32)],
        compiler_params=_cparams("arbitrary", "arbitrary"),
        name="gdn_chunk",
    )(qn, kn, vc, p1, bg, norm_w)


def _head_sum_matrix():
    return ((_iota((LANES, LANES), 0) // RWKV_HEAD) == (_iota((LANES, LANES), 1) // RWKV_HEAD)).astype(BF16)


def _rwkv_prep_kernel(r_ref, k_ref, v_ref, x_ref, rh_ref, kh_ref, vh_ref, xh_ref,
                      mur_ref, muk_ref, muv_ref, mux_ref, w0_ref, wup_ref, a0_ref, aup_ref, gup_ref,
                      kk_w_ref, ka_ref, rk_ref,
                      kkd_ref, rt_ref, kt_ref, bt_ref, kh_o_ref, bh_o_ref, v_o_ref, pc_ref, bonus_ref, g_ref,
                      *, tiles_per_seq):
    first = (pl.program_id(0) % tiles_per_seq) == 0
    c = CHUNK
    tm = r_ref.shape[0]

    def shift_mix(x_r, h_r, mu_r):
        x = x_r[...].astype(F32)
        halo = jnp.where(first, 0.0, h_r[...].astype(F32))
        prev = _shifted_rows(x, halo, 1)
        return x + (prev - x) * mu_r[...]

    r = shift_mix(r_ref, rh_ref, mur_ref)
    k = shift_mix(k_ref, kh_ref, muk_ref)
    v = shift_mix(v_ref, vh_ref, muv_ref)
    xs = shift_mix(x_ref, xh_ref, mux_ref)
    xw = xs[:, 0:LANES]
    lane = _iota(xw.shape, 1)
    tw = jnp.where(lane < RWKV_W_RANK, jnp.tanh(xw), 0.0)
    xa = jnp.where(lane >= RWKV_W_RANK, xw, 0.0)
    sg = _sigmoid(xs[:, LANES:2 * LANES])
    w_log = -_softplus(-(w0_ref[...] + _dot(tw, wup_ref[...]))) - 0.5
    lw = -jnp.exp(w_log)
    a = _sigmoid(a0_ref[...] + _dot(xa, aup_ref[...]))
    g = _dot(sg, gup_ref[...])
    hs = _head_sum_matrix()
    kkw = k * kk_w_ref[...]
    k2 = k * (1.0 + (a - 1.0) * ka_ref[...])
    rkr = r * k2 * rk_ref[...]
    kk_parts = []
    bonus_parts = []
    for blk in range(RWKV_WIDTH // LANES):
        cols = slice(blk * LANES, (blk + 1) * LANES)
        x_blk = kkw[:, cols]
        ss = _dot_exact_rhs(x_blk * x_blk, hs, 2)
        kk_parts.append(x_blk * lax.rsqrt(ss + 1e-6))
        bonus_parts.append(_dot_exact_rhs(rkr[:, cols], hs, 2) * v[:, cols])
    kk = jnp.concatenate(kk_parts, axis=1)
    bonus_ref[...] = jnp.concatenate(bonus_parts, axis=1).astype(bonus_ref.dtype)
    g_ref[...] = g.astype(g_ref.dtype)
    b = kk * a
    ri = _iota((tm, tm), 0)
    cj = _iota((tm, tm), 1)
    same = (ri // c) == (cj // c)
    gi = _dot_exact_lhs(jnp.where(same & (ri >= cj), 1.0, 0.0).astype(BF16), lw, 3)
    gl = _dot_exact_lhs(jnp.where(same, 1.0, 0.0).astype(BF16), lw, 3)
    ge = gi - lw
    e_neg = jnp.exp(-gi)
    e_tail = jnp.exp(gl - gi)
    kkd_ref[...] = (kk * jnp.exp(ge)).astype(BF16)
    rt_ref[...] = (r * jnp.exp(gi)).astype(BF16)
    kt_ref[...] = (k2 * e_neg).astype(BF16)
    bt_ref[...] = (b * e_neg).astype(BF16)
    kh_o_ref[...] = (k2 * e_tail).astype(BF16)
    bh_o_ref[...] = (b * e_tail).astype(BF16)
    v_o_ref[...] = v.astype(BF16)
    egl = jnp.exp(gl)
    for ci in range(tm // c):
        pc_ref[ci] = egl[ci * c:ci * c + 1, :]


def rwkv_prep(p1, p2, mu, w0, w_up, a0, a_up, g_up, k_k, k_a, r_k, layer, seq, tm):
    t = p1.shape[0]
    tm = min(tm, seq)
    w = RWKV_WIDTH
    cblk = RR_OFF // w
    hb = tm // SUBLANES
    xw2 = 2 * LANES

    def xspec(r):
        return pl.BlockSpec((tm, w), lambda i: (i, cblk + r))

    def hspec(r):
        return pl.BlockSpec((SUBLANES, w), lambda i: (jnp.maximum(i * hb - 1, 0), cblk + r))

    def row(width, blk):
        return pl.BlockSpec((None, 1, width), lambda i: (layer, 0, blk))

    zpad = jnp.zeros((w_up.shape[0], RWKV_W_RANK, w), F32)
    wup_p = jnp.concatenate([w_up, zpad], axis=1).astype(BF16)
    aup_p = jnp.concatenate([zpad, a_up], axis=1).astype(BF16)
    gup_b = g_up.astype(BF16)
    mu3 = mu[:, None, :]
    rk_row = r_k.reshape(r_k.shape[0], 1, w)
    outs = [jax.ShapeDtypeStruct((t, w), BF16)] * 7
    outs += [jax.ShapeDtypeStruct((t // CHUNK, 1, w), F32),
             jax.ShapeDtypeStruct((t, w), BF16), jax.ShapeDtypeStruct((t, w), BF16)]
    ospec = pl.BlockSpec((tm, w), lambda i: (i, 0))
    out_specs = [ospec] * 7 + [pl.BlockSpec((tm // CHUNK, 1, w), lambda i: (i, 0, 0)), ospec, ospec]
    return pl.pallas_call(
        functools.partial(_rwkv_prep_kernel, tiles_per_seq=seq // tm),
        grid=(t // tm,),
        in_specs=[xspec(0), xspec(1), xspec(2), pl.BlockSpec((tm, xw2), lambda i: (i, 0)),
                  hspec(0), hspec(1), hspec(2),
                  pl.BlockSpec((SUBLANES, xw2), lambda i: (jnp.maximum(i * hb - 1, 0), 0)),
                  row(w, 0), row(w, 1), row(w, 2),
                  pl.BlockSpec((None, 1, xw2), lambda i: (layer, 0, 3 * w // xw2)),
                  row(w, 0),
                  pl.BlockSpec((None, LANES, w), lambda i: (layer, 0, 0)),
                  row(w, 0),
                  pl.BlockSpec((None, LANES, w), lambda i: (layer, 0, 0)),
                  pl.BlockSpec((None, RWKV_G_RANK, w), lambda i: (layer, 0, 0)),
                  row(w, 0), row(w, 0), row(w, 0)],
        out_specs=out_specs,
        out_shape=outs,
        compiler_params=_cparams("arbitrary"),
        name="rwkv_prep",
    )(p1, p1, p1, p2, p1, p1, p1, p2, mu3, mu3, mu3, mu3, w0[:, None, :], wup_p, a0[:, None, :], aup_p,
      gup_b, k_k[:, None, :], k_a[:, None, :], rk_row)


def _rwkv_chunk_kernel(kkd_ref, rt_ref, kt_ref, bt_ref, kh_ref, bh_ref, v_ref, pc_ref, y_ref, st_ref):
    c = CHUNK

    @pl.when(pl.program_id(1) == 0)
    def _():
        st_ref[...] = jnp.zeros(st_ref.shape, F32)

    i64 = _iota((c, c), 0)
    j64 = _iota((c, c), 1)
    incl = i64 >= j64
    strict = i64 > j64
    lane = _iota((1, LANES), 1)
    bd = (_iota((LANES, LANES), 0) // RWKV_HEAD) == (_iota((LANES, LANES), 1) // RWKV_HEAD)
    for p in range(RWKV_WIDTH // LANES):
        cols = slice(p * LANES, (p + 1) * LANES)
        kkd = kkd_ref[:, cols]
        rt = rt_ref[:, cols]
        kt = kt_ref[:, cols]
        bt = bt_ref[:, cols]
        v = v_ref[:, cols]
        s0 = st_ref[p]
        l2 = jnp.concatenate([kkd, rt], axis=0)
        ks = _dot_nt(l2, s0)
        u = jnp.zeros((c, LANES), F32)
        a_rk = []
        a_rb = []
        masks = []
        for s in range(LANES // RWKV_HEAD):
            m = (lane // RWKV_HEAD) == s
            l2s = jnp.where(m, l2, jnp.zeros_like(l2))
            pk = _dot_nt(l2s, kt)
            pb = _dot_nt(l2s, bt)
            a_kk = jnp.where(strict, pk[:c], 0.0)
            a_kb = jnp.where(strict, pb[:c], 0.0)
            a_rk.append(jnp.where(incl, pk[c:], 0.0))
            a_rb.append(jnp.where(incl, pb[c:], 0.0))
            masks.append(m)
            tinv = _tri_inv(a_kb, c)
            u = jnp.where(m, _dot3(tinv, ks[:c] + _dot(a_kk, v)), u)
        ub = u.astype(BF16)
        y = jnp.zeros((c, LANES), F32)
        for s in range(LANES // RWKV_HEAD):
            y = jnp.where(masks[s], _dot(a_rk[s], v) - _dot(a_rb[s], ub), y)
        y_ref[:, cols] = ks[c:] + y
        x_t = jnp.concatenate([v, ub], axis=0)
        y2 = jnp.concatenate([kh_ref[:, cols], -bh_ref[:, cols]], axis=0)
        st_ref[p] = s0 * pc_ref[:, cols] + jnp.where(bd, _dot_tn(x_t, y2), 0.0)


def rwkv_chunk(tensors, pc, nb, seq):
    t, w = tensors[0].shape
    nck = seq // CHUNK
    xspec = pl.BlockSpec((CHUNK, w), lambda b, s: (b * nck + s, 0))
    return pl.pallas_call(
        _rwkv_chunk_kernel,
        grid=(nb, nck),
        in_specs=[xspec] * 7 + [pl.BlockSpec((None, 1, w), lambda b, s: (b * nck + s, 0, 0))],
        out_specs=xspec,
        out_shape=jax.ShapeDtypeStruct((t, w), F32),
        scratch_shapes=[pltpu.VMEM((w // LANES, LANES, LANES), F32)],
        compiler_params=_cparams("arbitrary", "arbitrary"),
        name="rwkv_chunk",
    )(*tensors, pc)


def _rwkv_post_kernel(y_ref, bonus_ref, g_ref, lw_ref, lb_ref, o_ref):
    hm = _head_sum_matrix() * (1.0 / RWKV_HEAD)
    hm = hm.astype(BF16)
    for blk in range(RWKV_WIDTH // LANES):
        cols = slice(blk * LANES, (blk + 1) * LANES)
        y = y_ref[:, cols]
        mean = _dot_exact_rhs(y, hm, 3)
        yc = y - mean
        var = _dot_exact_rhs(yc * yc, hm, 3)
        yn = yc * lax.rsqrt(var + RWKV_LN_EPS) * lw_ref[:, cols] + lb_ref[:, cols]
        out = (yn + bonus_ref[:, cols].astype(F32)) * g_ref[:, cols].astype(F32)
        o_ref[:, cols] = out.astype(o_ref.dtype)


def rwkv_post(y, bonus, g, lnx_w, lnx_b, layer, tm):
    t, w = y.shape
    tm = min(tm, t)
    xspec = pl.BlockSpec((tm, w), lambda i: (i, 0))
    rspec = pl.BlockSpec((None, 1, w), lambda i: (layer, 0, 0))
    return pl.pallas_call(
        _rwkv_post_kernel,
        grid=(t // tm,),
        in_specs=[xspec, xspec, xspec, rspec, rspec],
        out_specs=xspec,
        out_shape=jax.ShapeDtypeStruct((t, w), BF16),
        compiler_params=_cparams("arbitrary"),
        name="rwkv_post",
    )(y, bonus, g, lnx_w[:, None, :], lnx_b[:, None, :])


def _relayout_w_in(w_in):
    nl, d, _ = w_in.shape

    def cols(lo, width):
        return w_in[:, :, lo:lo + width]

    def zeros(width):
        return jnp.zeros((nl, d, width), w_in.dtype)

    kr = cols(_O_KR, MLA_ROPE)
    half = MLA_ROPE // 2
    kr_swap = jnp.concatenate([-kr[:, :, half:], kr[:, :, :half]], axis=-1)
    w1 = jnp.concatenate([
        cols(_O_CQ, MLA_Q_RANK),
        kr, zeros(LANES - MLA_ROPE), kr_swap, zeros(LANES - MLA_ROPE),
        cols(_O_CKV, MLA_KV_RANK), zeros(GQ_OFF - CKV_OFF - MLA_KV_RANK),
        cols(_O_GDN, 4096),
        cols(_O_RWKV, 3072),
        cols(_O_GATE, 3 * D_MODEL)], axis=-1).astype(BF16)
    w2 = jnp.concatenate([
        cols(_O_XW, 256), cols(_O_GB, 16), zeros(NP2 - 272)], axis=-1).astype(BF16)
    return w1, w2


def _relayout_w_uq(w_uq):
    nl, r, _ = w_uq.shape
    wq = w_uq.reshape(nl, r, MLA_HEADS, MLA_NOPE + MLA_ROPE)
    nope = wq[..., :MLA_NOPE]
    rope = wq[..., MLA_NOPE:]
    half = MLA_ROPE // 2
    swap = jnp.concatenate([-rope[..., half:], rope[..., :half]], axis=-1)
    z = jnp.zeros(rope.shape[:-1] + (LANES - MLA_ROPE,), w_uq.dtype)
    return jnp.concatenate([nope, rope, z, swap, z], axis=-1).reshape(nl, r, 3 * MLA_HEADS * LANES).astype(BF16)


def _relayout_w_ukv(w_ukv):
    nl, r, _ = w_ukv.shape
    wkv = w_ukv.reshape(nl, r, MLA_HEADS, MLA_NOPE + MLA_V)
    wk = wkv[..., :MLA_NOPE].reshape(nl, r, MLA_HEADS * MLA_NOPE)
    wv = wkv[..., MLA_NOPE:].reshape(nl, r, MLA_HEADS * MLA_V)
    return jnp.concatenate([wk, wv], axis=-1).astype(BF16)


def _rope_table(positions):
    inv = 1.0 / (ROPE_THETA ** (jnp.arange(0, MLA_ROPE, 2, dtype=F32) / MLA_ROPE))
    ang = positions.astype(F32)[..., None] * inv
    cos, sin = jnp.cos(ang), jnp.sin(ang)
    z = jnp.zeros(cos.shape[:-1] + (LANES - MLA_ROPE,), F32)
    cs = jnp.concatenate([cos, cos, z, sin, sin, z], axis=-1)
    return cs.reshape(-1, 2 * LANES)


def kernel(x, c, positions, w_ada, b_ada, norm1_w, w_in, mla_q_norm_w, mla_w_uq, mla_kv_norm_w, mla_w_ukv,
           gdn_conv_w, gdn_a_log, gdn_dt_bias, gdn_norm_w, rwkv_mu, rwkv_w0, rwkv_w_up, rwkv_a0, rwkv_a_up,
           rwkv_g_up, rwkv_k_k, rwkv_k_a, rwkv_r_k, rwkv_lnx_w, rwkv_lnx_b, w_branch, w_out, norm2_w,
           w_gate_up, w_down, final_norm_w):
    nb, seq, d = x.shape
    nl = w_in.shape[0]
    assert seq % (2 * CHUNK) == 0 and d == D_MODEL

    w1, w2 = _relayout_w_in(w_in)
    wq3 = _relayout_w_uq(mla_w_uq)
    wkv = _relayout_w_ukv(mla_w_ukv)
    w_branch_b = w_branch.astype(BF16)
    w_out_b = w_out.astype(BF16)
    w_gu_b = w_gate_up.astype(BF16)
    w_down_b = w_down.astype(BF16)
    cs = _rope_table(positions)

    mod = adaln_mod(c, w_ada, b_ada)
    modr = mod.reshape(nl * nb * 6, 1, d)
    n1 = norm1_w[:, None, :]
    n2 = norm2_w[:, None, :]
    qnw = mla_q_norm_w[:, None, :]
    kvnw = mla_kv_norm_w[:, None, :]
    gnw = gdn_norm_w[:, None, :]

    xf = x.reshape(nb * seq, d)
    for l in range(nl):
        p1 = norm_gemm(xf, n1, modr, w1, l, 0, nb, seq, BF16, 1024, 512)
        p2 = norm_gemm(xf, n1, modr, w2, l, 0, nb, seq, F32, 1024, NP2)
        q, k, v = mla_proj(p1, cs, qnw, kvnw, wq3, wkv, l, 512)
        o_a = flash_attn(q, k, v, nb, seq, 512)
        qn, kn, vc, bg = gdn_prep(p1, p2, gdn_conv_w, gdn_a_log, gdn_dt_bias, l, seq, 512)
        o_b = gdn_chunk(qn, kn, vc, p1, bg, gnw, l, nb, seq, 2 * CHUNK)
        prep = rwkv_prep(p1, p2, rwkv_mu, rwkv_w0, rwkv_w_up, rwkv_a0, rwkv_a_up, rwkv_g_up,
                         rwkv_k_k, rwkv_k_a, rwkv_r_k, l, seq, 512)
        y = rwkv_chunk(prep[:7], prep[7], nb, seq)
        o_c = rwkv_post(y, prep[8], prep[9], rwkv_lnx_w, rwkv_lnx_b, l, 512)
        merged = merge_gemm(o_a, o_b, o_c, w_branch_b, p1, l, 1024, 512)
        xf = resid_gemm(merged, w_out_b, xf, modr, l, 2, nb, seq, 1024, 512)
        act = ffn_up(xf, n2, modr, w_gu_b, l, nb, seq, 1024, 512)
        xf = resid_gemm(act, w_down_b, xf, modr, l, 5, nb, seq, 1024, 512)
    return final_norm(xf, final_norm_w, 512).reshape(nb, seq, d)
```

```python
import functools
import math

import jax
import jax.numpy as jnp
from jax import lax
from jax.experimental import pallas as pl
from jax.experimental.pallas import tpu as pltpu

F32 = jnp.float32
BF16 = jnp.bfloat16

D_MODEL = 2048
MLA_HEADS = 8
MLA_Q_RANK = 768
MLA_KV_RANK = 512
MLA_NOPE = 128
MLA_ROPE = 64
MLA_V = 128
ROPE_THETA = 10000.0
GDN_HEADS = 8
GDN_DK = 128
GDN_DV = 128
GDN_CONV = 4
RWKV_HEADS = 16
RWKV_HEAD = 64
RWKV_WIDTH = RWKV_HEADS * RWKV_HEAD
RWKV_W_RANK = 64
RWKV_A_RANK = 64
RWKV_G_RANK = 128
RWKV_LN_EPS = 64e-5
D_FF = 5632
NORM_EPS = 1e-6
CHUNK = 64
MXU_TILE = 256
GDN_STEP = MXU_TILE
RWKV_STEP = MXU_TILE // 2
GROUP = 4
FLASH_HEADS = 2
VT_PAD = 16
LANES = 128
SUBLANES = 8
VMEM_LIMIT = 56 * 1024 * 1024

CQ_OFF = 0
KR_OFF = 768
CKV_OFF = 1024
GQ_OFF = 2048
RR_OFF = 6144
GATE_OFF = 9216
NP1 = 15360
NP2 = 384

_O_CQ = 0
_O_CKV = 768
_O_KR = 1280
_O_GDN = 1344
_O_GZ = _O_GDN + 3072
_O_GB = _O_GZ + 1024
_O_GA = _O_GB + 8
_O_RWKV = _O_GA + 8
_O_XW = _O_RWKV + 3072
_O_XA = _O_XW + 64
_O_XG = _O_XA + 64
_O_GATE = _O_XG + 128


def _cparams(*sem):
    return pltpu.CompilerParams(dimension_semantics=sem, vmem_limit_bytes=VMEM_LIMIT)


def _dot(a, b):
    return jnp.dot(a.astype(BF16), b.astype(BF16), preferred_element_type=F32)


def _dot_nt(a, b):
    return lax.dot_general(a.astype(BF16), b.astype(BF16), (((1,), (1,)), ((), ())),
                           preferred_element_type=F32)


def _dot_tn(a, b):
    return lax.dot_general(a.astype(BF16), b.astype(BF16), (((0,), (0,)), ((), ())),
                           preferred_element_type=F32)


def _split(a, terms):
    parts = []
    rem = a
    for _ in range(terms):
        p = rem.astype(BF16)
        parts.append(p)
        rem = rem - p.astype(F32)
    return parts


def _dot_exact_lhs(a_bf, b, terms):
    out = None
    for p in _split(b, terms):
        t = _dot(a_bf, p)
        out = t if out is None else out + t
    return out


def _dot_exact_rhs(a, b_bf, terms):
    out = None
    for p in _split(a, terms):
        t = _dot(p, b_bf)
        out = t if out is None else out + t
    return out


def _iota(shape, dim):
    return lax.broadcasted_iota(jnp.int32, shape, dim)


def _tri_inv_m1(a_list, n):
    ns = [-a for a in a_list]
    ps = [_dot(a, a) for a in a_list]
    k = 2
    while True:
        pbs = [p.astype(BF16) for p in ps]
        ns = [nn + p + _dot(nn, pb) for nn, p, pb in zip(ns, ps, pbs)]
        k *= 2
        if k >= n:
            break
        ps = [_dot(pb, pb) for pb in pbs]
    return ns


def _sigmoid(x):
    return jax.nn.sigmoid(x)


def _softplus(x):
    return jnp.maximum(x, 0.0) + jnp.log1p(jnp.exp(-jnp.abs(x)))


def _mod_kernel(c_ref, w_ref, b_ref, o_ref):
    w = w_ref[...]
    for m in range(c_ref.shape[0]):
        c = c_ref[m]
        ca = c * _sigmoid(c)
        o_ref[m:m + 1, :] = jnp.sum(ca * w, axis=0, keepdims=True) + b_ref[...]


def adaln_mod(c, w_ada, b_ada):
    nl, d, n = w_ada.shape
    b = c.shape[0]
    tn = min(1024, n)
    return pl.pallas_call(
        _mod_kernel,
        grid=(nl, n // tn),
        in_specs=[pl.BlockSpec((b, d, 1), lambda l, j: (0, 0, 0)),
                  pl.BlockSpec((None, d, tn), lambda l, j: (l, 0, j)),
                  pl.BlockSpec((None, 1, tn), lambda l, j: (l, 0, j))],
        out_specs=pl.BlockSpec((None, b, tn), lambda l, j: (l, 0, j)),
        out_shape=jax.ShapeDtypeStruct((nl, b, n), F32),
        compiler_params=_cparams("arbitrary", "arbitrary"),
        name="adaln_mod",
    )(c[:, :, None], w_ada, b_ada[:, None, :])


def _modulate_to(h_scr, x_ref, nw_ref, sh_ref, sc_ref):
    x = x_ref[...]
    ms = jnp.mean(x * x, axis=-1, keepdims=True)
    y = x * lax.rsqrt(ms + NORM_EPS) * nw_ref[...]
    h_scr[...] = (y * (1.0 + sc_ref[...]) + sh_ref[...]).astype(BF16)


def _norm_gemm_kernel(x_ref, nw_ref, sh_ref, sc_ref, w_ref, o_ref, h_scr):
    @pl.when(pl.program_id(1) == 0)
    def _():
        _modulate_to(h_scr, x_ref, nw_ref, sh_ref, sc_ref)

    o_ref[...] = jnp.dot(h_scr[...], w_ref[...], preferred_element_type=F32).astype(o_ref.dtype)


def _mod_spec(d, layer, nb, which, tiles_per_seq):
    return pl.BlockSpec((None, 1, d),
                        lambda i, j: ((layer * nb + i // tiles_per_seq) * 6 + which, 0, 0))


def norm_gemm(x, norm_w, modr, w, layer, which_sh, nb, seq, out_dtype, tm, tn):
    t, d = x.shape
    n = w.shape[-1]
    tm = min(tm, seq)
    tn = min(tn, n)
    tps = seq // tm
    return pl.pallas_call(
        _norm_gemm_kernel,
        grid=(t // tm, n // tn),
        in_specs=[pl.BlockSpec((tm, d), lambda i, j: (i, 0)),
                  pl.BlockSpec((None, 1, d), lambda i, j: (layer, 0, 0)),
                  _mod_spec(d, layer, nb, which_sh, tps),
                  _mod_spec(d, layer, nb, which_sh + 1, tps),
                  pl.BlockSpec((None, d, tn), lambda i, j: (layer, 0, j))],
        out_specs=pl.BlockSpec((tm, tn), lambda i, j: (i, j)),
        out_shape=jax.ShapeDtypeStruct((t, n), out_dtype),
        scratch_shapes=[pltpu.VMEM((tm, d), BF16)],
        compiler_params=_cparams("arbitrary", "arbitrary"),
        name="norm_gemm",
    )(x, norm_w, modr, modr, w)


def _ffn_up_kernel(x_ref, nw_ref, sh_ref, sc_ref, wg_ref, wu_ref, o_ref, h_scr):
    @pl.when(pl.program_id(1) == 0)
    def _():
        _modulate_to(h_scr, x_ref, nw_ref, sh_ref, sc_ref)

    h = h_scr[...]
    gate = jnp.dot(h, wg_ref[...], preferred_element_type=F32)
    up = jnp.dot(h, wu_ref[...], preferred_element_type=F32)
    o_ref[...] = (gate * _sigmoid(gate) * up).astype(o_ref.dtype)


def ffn_up(x, norm_w, modr, w_gu, layer, nb, seq, tm, tn):
    t, d = x.shape
    f = w_gu.shape[-1] // 2
    tm = min(tm, seq)
    tn = min(tn, f)
    tps = seq // tm
    nf = f // tn
    return pl.pallas_call(
        _ffn_up_kernel,
        grid=(t // tm, nf),
        in_specs=[pl.BlockSpec((tm, d), lambda i, j: (i, 0)),
                  pl.BlockSpec((None, 1, d), lambda i, j: (layer, 0, 0)),
                  _mod_spec(d, layer, nb, 3, tps),
                  _mod_spec(d, layer, nb, 4, tps),
                  pl.BlockSpec((None, d, tn), lambda i, j: (layer, 0, j)),
                  pl.BlockSpec((None, d, tn), lambda i, j: (layer, 0, j + nf))],
        out_specs=pl.BlockSpec((tm, tn), lambda i, j: (i, j)),
        out_shape=jax.ShapeDtypeStruct((t, f), BF16),
        scratch_shapes=[pltpu.VMEM((tm, d), BF16)],
        compiler_params=_cparams("arbitrary", "arbitrary"),
        name="ffn_up",
    )(x, norm_w, modr, modr, w_gu, w_gu)


def _resid_gemm_kernel(a_ref, w_ref, x_ref, gt_ref, o_ref):
    y = jnp.dot(a_ref[...], w_ref[...], preferred_element_type=F32)
    o_ref[...] = x_ref[...] + gt_ref[...] * y


def resid_gemm(a, w, x, modr, layer, which_gt, nb, seq, tm, tn):
    t, k = a.shape
    d = x.shape[-1]
    tm = min(tm, seq)
    tn = min(tn, d)
    tps = seq // tm
    return pl.pallas_call(
        _resid_gemm_kernel,
        grid=(t // tm, d // tn),
        in_specs=[pl.BlockSpec((tm, k), lambda i, j: (i, 0)),
                  pl.BlockSpec((None, k, tn), lambda i, j: (layer, 0, j)),
                  pl.BlockSpec((tm, tn), lambda i, j: (i, j)),
                  pl.BlockSpec((None, 1, tn),
                               lambda i, j: ((layer * nb + i // tps) * 6 + which_gt, 0, j))],
        out_specs=pl.BlockSpec((tm, tn), lambda i, j: (i, j)),
        out_shape=jax.ShapeDtypeStruct((t, d), F32),
        compiler_params=_cparams("arbitrary", "arbitrary"),
        name="resid_gemm",
    )(a, w, x, modr)


def _merge_kernel(oa_ref, ob_ref, oc_ref, wa_ref, wb_ref, wc_ref, ga_ref, gb_ref, gc_ref, o_ref):
    def branch(o_r, w_r, g_r):
        y = jnp.dot(o_r[...], w_r[...], preferred_element_type=F32)
        return _sigmoid(g_r[...].astype(F32)) * y

    acc = branch(oa_ref, wa_ref, ga_ref) + branch(ob_ref, wb_ref, gb_ref) + branch(oc_ref, wc_ref, gc_ref)
    o_ref[...] = acc.astype(o_ref.dtype)


def merge_gemm(o_a, o_b, o_c, w_branch, p1, layer, tm, tn):
    t, kb = o_a.shape
    d = w_branch.shape[-1]
    tm = min(tm, t)
    tn = min(tn, d)
    gblk = GATE_OFF // tn
    dblk = d // tn

    def ospec():
        return pl.BlockSpec((tm, kb), lambda i, j: (i, 0))

    def wspec(r):
        return pl.BlockSpec((None, kb, tn), lambda i, j: (layer, r, j))

    def gspec(r):
        return pl.BlockSpec((tm, tn), lambda i, j: (i, gblk + r * dblk + j))

    return pl.pallas_call(
        _merge_kernel,
        grid=(t // tm, d // tn),
        in_specs=[ospec(), ospec(), ospec(), wspec(0), wspec(1), wspec(2), gspec(0), gspec(1), gspec(2)],
        out_specs=pl.BlockSpec((tm, tn), lambda i, j: (i, j)),
        out_shape=jax.ShapeDtypeStruct((t, d), BF16),
        compiler_params=_cparams("arbitrary", "arbitrary"),
        name="merge_gemm",
    )(o_a, o_b, o_c, w_branch, w_branch, w_branch, p1, p1, p1)


def _final_norm_kernel(x_ref, w_ref, o_ref):
    x = x_ref[...]
    ms = jnp.mean(x * x, axis=-1, keepdims=True)
    o_ref[...] = x * lax.rsqrt(ms + NORM_EPS) * w_ref[...]


def final_norm(x, w, tm):
    t, d = x.shape
    tm = min(tm, t)
    return pl.pallas_call(
        _final_norm_kernel,
        grid=(t // tm,),
        in_specs=[pl.BlockSpec((tm, d), lambda i: (i, 0)), pl.BlockSpec((1, d), lambda i: (0, 0))],
        out_specs=pl.BlockSpec((tm, d), lambda i: (i, 0)),
        out_shape=jax.ShapeDtypeStruct((t, d), F32),
        compiler_params=_cparams("arbitrary"),
        name="final_norm",
    )(x, w[None, :])


def _rms_rows(x, w):
    ms = jnp.mean(x * x, axis=-1, keepdims=True)
    return x * lax.rsqrt(ms + NORM_EPS) * w


def _mla_proj_kernel(cq_ref, kr_ref, ckv_ref, cs_ref, qnw_ref, kvnw_ref, wq_ref, wkv_ref,
                     q_ref, k_ref, vt_ref):
    scale =(MLA_NOPE + MLA_ROPE) ** -0.5 * math.log2(math.e)
    cc = cs_ref[:, 0:LANES]
    ss = cs_ref[:, LANES:2 * LANES]
    nq = _rms_rows(cq_ref[...].astype(F32), qnw_ref[...]).astype(BF16)
    q3 = jnp.dot(nq, wq_ref[...], preferred_element_type=F32)
    for h in range(MLA_HEADS):
        b0 = h * 3 * LANES
        nope = q3[:, b0:b0 + LANES]
        rope = q3[:, b0 + LANES:b0 + 2 * LANES] * cc + q3[:, b0 + 2 * LANES:b0 + 3 * LANES] * ss
        q_ref[:, 2 * h * LANES:(2 * h + 1) * LANES] = (nope * scale).astype(BF16)
        q_ref[:, (2 * h + 1) * LANES:(2 * h + 2) * LANES] = (rope * scale).astype(BF16)
    nkv = _rms_rows(ckv_ref[...].astype(F32), kvnw_ref[...]).astype(BF16)
    kv = jnp.dot(nkv, wkv_ref[...], preferred_element_type=F32)
    kr = kr_ref[...].astype(F32)
    krope = (kr[:, 0:LANES] * cc + kr[:, LANES:2 * LANES] * ss).astype(BF16)
    for h in range(MLA_HEADS):
        k_ref[:, 2 * h * LANES:(2 * h + 1) * LANES] = kv[:, h * LANES:(h + 1) * LANES].astype(BF16)
        k_ref[:, (2 * h + 1) * LANES:(2 * h + 2) * LANES] = krope
    ones = jnp.ones((VT_PAD, kv.shape[0]), BF16)
    for h in range(MLA_HEADS):
        vh = kv[:, (MLA_HEADS + h) * LANES:(MLA_HEADS + h + 1) * LANES]
        vt_ref[h, 0:MLA_V, :] = vh.T.astype(BF16)
        vt_ref[h, MLA_V:MLA_V + VT_PAD, :] = ones


def mla_proj(p1, cs, q_norm_w, kv_norm_w, wq3, wkv, layer, nb, seq, tm):
    t = p1.shape[0]
    tm = min(tm, seq)
    tps = seq // tm
    hq = MLA_HEADS * 2 * LANES
    return pl.pallas_call(
        _mla_proj_kernel,
        grid=(t // tm,),
        in_specs=[pl.BlockSpec((tm, MLA_Q_RANK), lambda i: (i, CQ_OFF // MLA_Q_RANK)),
                  pl.BlockSpec((tm, 2 * LANES), lambda i: (i, KR_OFF // (2 * LANES))),
                  pl.BlockSpec((tm, MLA_KV_RANK), lambda i: (i, CKV_OFF // MLA_KV_RANK)),
                  pl.BlockSpec((tm, 2 * LANES), lambda i: (i, 0)),
                  pl.BlockSpec((None, 1, MLA_Q_RANK), lambda i: (layer, 0, 0)),
                  pl.BlockSpec((None, 1, MLA_KV_RANK), lambda i: (layer, 0, 0)),
                  pl.BlockSpec((None, MLA_Q_RANK, 3 * MLA_HEADS * LANES), lambda i: (layer, 0, 0)),
                  pl.BlockSpec((None, MLA_KV_RANK, 2 * MLA_HEADS * LANES), lambda i: (layer, 0, 0))],
        out_specs=[pl.BlockSpec((tm, hq), lambda i: (i, 0)),
                   pl.BlockSpec((tm, hq), lambda i: (i, 0)),
                   pl.BlockSpec((None, MLA_HEADS, MLA_V + VT_PAD, tm), lambda i: (i // tps, 0, 0, i % tps))],
        out_shape=[jax.ShapeDtypeStruct((t, hq), BF16),
                   jax.ShapeDtypeStruct((t, hq), BF16),
                   jax.ShapeDtypeStruct((nb, MLA_HEADS, MLA_V + VT_PAD, seq), BF16)],
        compiler_params=_cparams("arbitrary"),
        name="mla_proj",
    )(p1, p1, p1, cs, q_norm_w, kv_norm_w, wq3, wkv)


def _flash_kernel(it_ref, jt_ref, q_ref, k_ref, vt_ref, o_ref, m_scr, acc_scr):
    t = pl.program_id(2)
    i = it_ref[t]
    j = jt_ref[t]
    heads = range(FLASH_HEADS)

    @pl.when(j == 0)
    def _():
        m_scr[...] = jnp.full(m_scr.shape, -jnp.inf, F32)
        acc_scr[...] = jnp.zeros(acc_scr.shape, F32)

    def step(masked):
        sts = [lax.dot_general(k_ref[:, 2 * g * LANES:2 * (g + 1) * LANES],
                               q_ref[:, 2 * g * LANES:2 * (g + 1) * LANES],
                               (((1,), (1,)), ((), ())), preferred_element_type=F32) for g in heads]
        if masked:
            keep = _iota(sts[0].shape, 1) >= _iota(sts[0].shape, 0)
            sts = [jnp.where(keep, s, -jnp.inf) for s in sts]
        m_prev = [m_scr[g] for g in heads]
        m_new = [jnp.maximum(mp, jnp.max(s, axis=0, keepdims=True)) for mp, s in zip(m_prev, sts)]
        pts = [jnp.exp2(s - mn).astype(BF16) for s, mn in zip(sts, m_new)]
        alphas = [jnp.exp2(mp - mn) for mp, mn in zip(m_prev, m_new)]
        pvs = [jnp.dot(vt_ref[g], pt, preferred_element_type=F32) for g, pt in zip(heads, pts)]
        for g in heads:
            acc_scr[g] = alphas[g] * acc_scr[g] + pvs[g]
            m_scr[g] = m_new[g]

    @pl.when(j < i)
    def _():
        step(False)

    @pl.when(j == i)
    def _():
        step(True)
        for g in heads:
            acc = acc_scr[g]
            o_t = acc[0:MLA_V, :] / acc[MLA_V:MLA_V + 1, :]
            o_ref[:, g * LANES:(g + 1) * LANES] = o_t.T.astype(o_ref.dtype)


def flash_attn(q, k, vt, nb, seq, tq):
    tq = min(tq, seq)
    nq = seq // tq
    hg = FLASH_HEADS
    pairs = [(i, j) for i in range(nq) for j in range(i + 1)]
    it = jnp.asarray([p[0] for p in pairs], jnp.int32)
    jt = jnp.asarray([p[1] for p in pairs], jnp.int32)
    grid_spec = pltpu.PrefetchScalarGridSpec(
        num_scalar_prefetch=2,
        grid=(nb, MLA_HEADS // hg, len(pairs)),
        in_specs=[pl.BlockSpec((tq, hg * 2 * LANES), lambda b, h, t, it_r, jt_r: (b * nq + it_r[t], h)),
                  pl.BlockSpec((tq, hg * 2 * LANES), lambda b, h, t, it_r, jt_r: (b * nq + jt_r[t], h)),
                  pl.BlockSpec((None, hg, MLA_V + VT_PAD, tq), lambda b, h, t, it_r, jt_r: (b, h, 0, jt_r[t]))],
        out_specs=pl.BlockSpec((tq, hg * LANES), lambda b, h, t, it_r, jt_r: (b * nq + it_r[t], h)),
        scratch_shapes=[pltpu.VMEM((hg, 1, tq), F32), pltpu.VMEM((hg, MLA_V + VT_PAD, tq), F32)],
    )
    return pl.pallas_call(
        _flash_kernel,
        grid_spec=grid_spec,
        out_shape=jax.ShapeDtypeStruct((nb * seq, MLA_HEADS * LANES), BF16),
        compiler_params=_cparams("arbitrary", "arbitrary", "arbitrary"),
        name="flash_attn",
    )(it, jt, q, k, vt)


def _shifted_rows(x, halo, d):
    xr = pltpu.roll(x, d, 0)
    hr = pltpu.roll(halo, d, 0)
    top = jnp.where(_iota(halo.shape, 0) < d, hr, xr[0:SUBLANES])
    return jnp.concatenate([top, xr[SUBLANES:]], axis=0)


def _gdn_prep_kernel(q_ref, k_ref, v_ref, qh_ref, kh_ref, vh_ref, cwq_ref, cwk_ref, cwv_ref,
                     p2_ref, alog_ref, dtb_ref, qo_ref, ko_ref, vo_ref, bg_ref, *, tiles_per_seq):
    first = (pl.program_id(0) % tiles_per_seq) == 0

    def conv_silu(x_ref, h_ref, w_ref):
        x = x_ref[...].astype(F32)
        halo = jnp.where(first, 0.0, h_ref[...].astype(F32))
        w = w_ref[...]
        acc = x * w[GDN_CONV - 1:GDN_CONV, :]
        for d in range(1, GDN_CONV):
            acc = acc + _shifted_rows(x, halo, d) * w[GDN_CONV - 1 - d:GDN_CONV - d, :]
        return acc * _sigmoid(acc)

    def l2n(x, mult):
        outs = []
        for h in range(GDN_HEADS):
            xh = x[:, h * LANES:(h + 1) * LANES]
            ss = jnp.sum(xh * xh, axis=-1, keepdims=True)
            outs.append(xh * (lax.rsqrt(ss + 1e-6) * mult))
        return jnp.concatenate(outs, axis=1)

    qo_ref[...] = l2n(conv_silu(q_ref, qh_ref, cwq_ref), GDN_DK ** -0.5).astype(BF16)
    ko_ref[...] = l2n(conv_silu(k_ref, kh_ref, cwk_ref), 1.0).astype(BF16)
    vo_ref[...] = conv_silu(v_ref, vh_ref, cwv_ref).astype(BF16)
    p2 = p2_ref[...]
    lane = _iota(p2.shape, 1)
    g = -jnp.exp(alog_ref[...]) * _softplus(p2 + dtb_ref[...])
    bg_ref[...] = jnp.where(lane < GDN_HEADS, _sigmoid(p2), g)


def gdn_prep(p1, p2, conv_w, a_log, dt_bias, layer, seq, tm):
    t = p1.shape[0]
    tm = min(tm, seq)
    w = GDN_HEADS * GDN_DK
    cblk = GQ_OFF // w
    hb = tm // SUBLANES

    def xspec(r):
        return pl.BlockSpec((tm, w), lambda i: (i, cblk + r))

    def hspec(r):
        return pl.BlockSpec((SUBLANES, w), lambda i: (jnp.maximum(i * hb - 1, 0), cblk + r))

    def cwspec(r):
        return pl.BlockSpec((None, GDN_CONV, w), lambda i: (layer, 0, r))

    pad = jnp.zeros((LANES - 2 * GDN_HEADS,), F32)
    z8 = jnp.zeros((GDN_HEADS,), F32)
    alog_row = jnp.concatenate([z8, a_log[layer], pad])[None, :]
    dtb_row = jnp.concatenate([z8, dt_bias[layer], pad])[None, :]
    return pl.pallas_call(
        functools.partial(_gdn_prep_kernel, tiles_per_seq=seq // tm),
        grid=(t // tm,),
        in_specs=[xspec(0), xspec(1), xspec(2), hspec(0), hspec(1), hspec(2),
                  cwspec(0), cwspec(1), cwspec(2),
                  pl.BlockSpec((tm, LANES), lambda i: (i, 2)),
                  pl.BlockSpec((1, LANES), lambda i: (0, 0)),
                  pl.BlockSpec((1, LANES), lambda i: (0, 0))],
        out_specs=[pl.BlockSpec((tm, w), lambda i: (i, 0))] * 3 + [pl.BlockSpec((tm, LANES), lambda i: (i, 0))],
        out_shape=[jax.ShapeDtypeStruct((t, w), BF16)] * 3 + [jax.ShapeDtypeStruct((t, LANES), F32)],
        compiler_params=_cparams("arbitrary"),
        name="gdn_prep",
    )(p1, p1, p1, p1, p1, p1, conv_w, conv_w, conv_w, p2, alog_row, dtb_row)


def _gdn_chunk_kernel(q_ref, k_ref, v_ref, z_ref, bg_ref, nw_ref, o_ref, st_ref, uw_scr, qk_scr, qg_scr, kg_scr):
    nb, tb, _ = q_ref.shape
    c = CHUNK
    nck = tb // c

    @pl.when(pl.program_id(0) == 0)
    def _():
        st_ref[...] = jnp.zeros(st_ref.shape, F32)

    r = _iota((tb, tb), 0)
    cc = _iota((tb, tb), 1)
    same = (r // c) == (cc // c)
    incl = same & (r >= cc)
    strict = same & (r > cc)
    units = [(b, h) for b in range(nb) for h in range(GDN_HEADS)]
    lmat = jnp.where(incl, 1.0, 0.0).astype(BF16)
    smat = jnp.where(same, 1.0, 0.0).astype(BF16)
    g_cum = [_dot_exact_lhs(lmat, bg_ref[b], 3) for b in range(nb)]
    g_tot = [_dot_exact_lhs(smat, bg_ref[b], 3) for b in range(nb)]
    g_cum_t = [g.T for g in g_cum]

    def cols(h):
        return slice(h * LANES, (h + 1) * LANES)

    def glane(h):
        return slice(GDN_HEADS + h, GDN_HEADS + h + 1)

    for g0 in range(0, len(units), GROUP):
        grp = units[g0:g0 + GROUP]
        ks = [k_ref[b, :, cols(h)] for b, h in grp]
        gcs = [g_cum[b][:, glane(h)] for b, h in grp]
        betas = [bg_ref[b, :, h:h + 1] for b, h in grp]
        decs = [jnp.exp(jnp.where(incl, gc - g_cum_t[b][glane(h), :], -jnp.inf))
                for gc, (b, h) in zip(gcs, grp)]
        a_s = [jnp.where(strict, _dot_nt(k, k) * dec, 0.0) * beta for k, dec, beta in zip(ks, decs, betas)]
        ns = _tri_inv_m1(a_s, c)
        for idx, (b, h) in enumerate(grp):
            u = g0 + idx
            q = q_ref[b, :, cols(h)]
            kf = ks[idx].astype(F32)
            vf = v_ref[b, :, cols(h)].astype(F32)
            eg = jnp.exp(gcs[idx])
            rhs = jnp.concatenate([betas[idx] * vf, (betas[idx] * eg) * kf], axis=1)
            uw_scr[u] = rhs + _dot(ns[idx], rhs)
            qk_scr[u] = (_dot_nt(q, ks[idx]) * decs[idx]).astype(BF16)
            qg_scr[u] = (q.astype(F32) * eg).astype(BF16)
            kg_scr[u] = (kf * jnp.exp(g_tot[b][:, glane(h)] - gcs[idx])).astype(BF16)

    states = [st_ref[u] for u in range(len(units))]
    w_parts = [[] for _ in units]
    og_parts = [[] for _ in units]
    for ci in range(nck):
        rows = slice(ci * c, (ci + 1) * c)
        sbs = [s.astype(BF16) for s in states]
        wqs = [_dot(jnp.concatenate([uw_scr[u, rows, GDN_DV:].astype(BF16), qg_scr[u, rows, :]], axis=0), sbs[u])
               for u in range(len(units))]
        for u in range(len(units)):
            w_parts[u].append((uw_scr[u, rows, :GDN_DV] - wqs[u][:c]).astype(BF16))
            og_parts[u].append(wqs[u][c:])
        states = [jnp.exp(g_tot[b][ci * c:ci * c + 1, glane(h)]) * states[u]
                  + _dot_tn(kg_scr[u, rows, :], w_parts[u][ci])
                  for u, (b, h) in enumerate(units)]

    nw = nw_ref[...]
    for u, (b, h) in enumerate(units):
        o = jnp.concatenate(og_parts[u], axis=0) + _dot(qk_scr[u], jnp.concatenate(w_parts[u], axis=0))
        zf = z_ref[b, :, cols(h)].astype(F32)
        o_ref[b, :, cols(h)] = (_rms_rows(o, nw) * (zf * _sigmoid(zf))).astype(o_ref.dtype)
        st_ref[u] = states[u]


def gdn_chunk(qn, kn, vc, p1, bg, norm_w, layer, nb, seq):
    t, w = qn.shape
    tb = min(GDN_STEP, seq)
    zblk = (GQ_OFF + 3 * w) // w
    nu = nb * GDN_HEADS

    def xspec(cb):
        return pl.BlockSpec((nb, tb, w), lambda s: (0, s, cb))

    out = pl.pallas_call(
        _gdn_chunk_kernel,
        grid=(seq // tb,),
        in_specs=[xspec(0), xspec(0), xspec(0), xspec(zblk),
                  pl.BlockSpec((nb, tb, LANES), lambda s: (0, s, 0)),
                  pl.BlockSpec((None, 1, GDN_DV), lambda s: (layer, 0, 0))],
        out_specs=xspec(0),
        out_shape=jax.ShapeDtypeStruct((nb, seq, w), BF16),
        scratch_shapes=[pltpu.VMEM((nu, GDN_DK, GDN_DV), F32),
                        pltpu.VMEM((nu, tb, 2 * GDN_DV), F32),
                        pltpu.VMEM((nu, tb, tb), BF16),
                        pltpu.VMEM((nu, tb, GDN_DK), BF16),
                        pltpu.VMEM((nu, tb, GDN_DK), BF16)],
        compiler_params=_cparams("arbitrary"),
        name="gdn_chunk",
    )(qn.reshape(nb, seq, w), kn.reshape(nb, seq, w), vc.reshape(nb, seq, w),
      p1.reshape(nb, seq, p1.shape[-1]), bg.reshape(nb, seq, LANES), norm_w)
    return out.reshape(t, w)


def _head_sum_matrix():
    return ((_iota((LANES, LANES), 0) // RWKV_HEAD) == (_iota((LANES, LANES), 1) // RWKV_HEAD)).astype(BF16)


def _rwkv_prep_kernel(r_ref, k_ref, v_ref, x_ref, rh_ref, kh_ref, vh_ref, xh_ref,
                      mur_ref, muk_ref, muv_ref, mux_ref, w0_ref, wup_ref, a0_ref, aup_ref, gup_ref,
                      kk_w_ref, ka_ref, rk_ref,
                      kkd_ref, rt_ref, kt_ref, bt_ref, kh_o_ref, bh_o_ref, v_o_ref, pc_ref, bonus_ref, g_ref,
                      *, tiles_per_seq):
    first = (pl.program_id(0) % tiles_per_seq) == 0
    c = CHUNK
    tm = r_ref.shape[0]

    def shift_mix(x_r, h_r, mu_r):
        x = x_r[...].astype(F32)
        halo = jnp.where(first, 0.0, h_r[...].astype(F32))
        prev = _shifted_rows(x, halo, 1)
        return x + (prev - x) * mu_r[...]

    r = shift_mix(r_ref, rh_ref, mur_ref)
    k = shift_mix(k_ref, kh_ref, muk_ref)
    v = shift_mix(v_ref, vh_ref, muv_ref)
    xs = shift_mix(x_ref, xh_ref, mux_ref)
    xw = xs[:, 0:LANES]
    lane = _iota(xw.shape, 1)
    tw = jnp.where(lane < RWKV_W_RANK, jnp.tanh(xw), 0.0)
    xa = jnp.where(lane >= RWKV_W_RANK, xw, 0.0)
    sg = _sigmoid(xs[:, LANES:2 * LANES])
    w_log = -_softplus(-(w0_ref[...] + _dot(tw, wup_ref[...]))) - 0.5
    lw = -jnp.exp(w_log)
    a = _sigmoid(a0_ref[...] + _dot(xa, aup_ref[...]))
    g = _dot(sg, gup_ref[...])
    hs = _head_sum_matrix()
    kkw = k * kk_w_ref[...]
    k2 = k * (1.0 + (a - 1.0) * ka_ref[...])
    rkr = r * k2 * rk_ref[...]
    kk_parts = []
    bonus_parts = []
    for blk in range(RWKV_WIDTH // LANES):
        cols = slice(blk * LANES, (blk + 1) * LANES)
        x_blk = kkw[:, cols]
        ss = _dot_exact_rhs(x_blk * x_blk, hs, 2)
        kk_parts.append(x_blk * lax.rsqrt(ss + 1e-6))
        bonus_parts.append(_dot_exact_rhs(rkr[:, cols], hs, 2) * v[:, cols])
    kk = jnp.concatenate(kk_parts, axis=1)
    bonus_ref[...] = jnp.concatenate(bonus_parts, axis=1).astype(bonus_ref.dtype)
    g_ref[...] = g.astype(g_ref.dtype)
    b = kk * a
    ri = _iota((tm, tm), 0)
    cj = _iota((tm, tm), 1)
    same = (ri // c) == (cj // c)
    gi = _dot_exact_lhs(jnp.where(same & (ri >= cj), 1.0, 0.0).astype(BF16), lw, 3)
    gl = _dot_exact_lhs(jnp.where(same, 1.0, 0.0).astype(BF16), lw, 3)
    ge = gi - lw
    e_neg = jnp.exp(-gi)
    e_tail = jnp.exp(gl - gi)
    kkd_ref[...] = (kk * jnp.exp(ge)).astype(BF16)
    rt_ref[...] = (r * jnp.exp(gi)).astype(BF16)
    kt_ref[...] = (k2 * e_neg).astype(BF16)
    bt_ref[...] = (b * e_neg).astype(BF16)
    kh_o_ref[...] = (k2 * e_tail).astype(BF16)
    bh_o_ref[...] = (b * e_tail).astype(BF16)
    v_o_ref[...] = v.astype(BF16)
    egl = jnp.exp(gl)
    for ci in range(tm // c):
        pc_ref[ci] = egl[ci * c:ci * c + 1, :]


def rwkv_prep(p1, p2, mu, w0, w_up, a0, a_up, g_up, k_k, k_a, r_k, layer, seq, tm):
    t = p1.shape[0]
    tm = min(tm, seq)
    w = RWKV_WIDTH
    cblk = RR_OFF // w
    hb = tm // SUBLANES
    xw2 = 2 * LANES

    def xspec(r):
        return pl.BlockSpec((tm, w), lambda i: (i, cblk + r))

    def hspec(r):
        return pl.BlockSpec((SUBLANES, w), lambda i: (jnp.maximum(i * hb - 1, 0), cblk + r))

    def row(width, blk):
        return pl.BlockSpec((None, 1, width), lambda i: (layer, 0, blk))

    zpad = jnp.zeros((w_up.shape[0], RWKV_W_RANK, w), F32)
    wup_p = jnp.concatenate([w_up, zpad], axis=1).astype(BF16)
    aup_p = jnp.concatenate([zpad, a_up], axis=1).astype(BF16)
    gup_b = g_up.astype(BF16)
    mu3 = mu[:, None, :]
    rk_row = r_k.reshape(r_k.shape[0], 1, w)
    outs = [jax.ShapeDtypeStruct((t, w), BF16)] * 7
    outs += [jax.ShapeDtypeStruct((t // CHUNK, 1, w), F32),
             jax.ShapeDtypeStruct((t, w), BF16), jax.ShapeDtypeStruct((t, w), BF16)]
    ospec = pl.BlockSpec((tm, w), lambda i: (i, 0))
    out_specs = [ospec] * 7 + [pl.BlockSpec((tm // CHUNK, 1, w), lambda i: (i, 0, 0)), ospec, ospec]
    return pl.pallas_call(
        functools.partial(_rwkv_prep_kernel, tiles_per_seq=seq // tm),
        grid=(t // tm,),
        in_specs=[xspec(0), xspec(1), xspec(2), pl.BlockSpec((tm, xw2), lambda i: (i, 0)),
                  hspec(0), hspec(1), hspec(2),
                  pl.BlockSpec((SUBLANES, xw2), lambda i: (jnp.maximum(i * hb - 1, 0), 0)),
                  row(w, 0), row(w, 1), row(w, 2),
                  pl.BlockSpec((None, 1, xw2), lambda i: (layer, 0, 3 * w // xw2)),
                  row(w, 0),
                  pl.BlockSpec((None, LANES, w), lambda i: (layer, 0, 0)),
                  row(w, 0),
                  pl.BlockSpec((None, LANES, w), lambda i: (layer, 0, 0)),
                  pl.BlockSpec((None, RWKV_G_RANK, w), lambda i: (layer, 0, 0)),
                  row(w, 0), row(w, 0), row(w, 0)],
        out_specs=out_specs,
        out_shape=outs,
        compiler_params=_cparams("arbitrary"),
        name="rwkv_prep",
    )(p1, p1, p1, p2, p1, p1, p1, p2, mu3, mu3, mu3, mu3, w0[:, None, :], wup_p, a0[:, None, :], aup_p,
      gup_b, k_k[:, None, :], k_a[:, None, :], rk_row)


def _rwkv_chunk_kernel(kkd_ref, rt_ref, kt_ref, bt_ref, kh_ref, bh_ref, v_ref, pc_ref, y_ref, st_ref,
                       n_scr, arb_scr, uloc_scr, yloc_scr):
    nb, tb, w = kkd_ref.shape
    c = CHUNK
    nck = tb // c
    n2 = 2 * tb
    npair = w // LANES

    @pl.when(pl.program_id(0) == 0)
    def _():
        st_ref[...] = jnp.zeros(st_ref.shape, F32)

    r = _iota((n2, n2), 0)
    cc = _iota((n2, n2), 1)
    same = (r // c) == (cc // c)
    incl = same & (r >= cc)
    strict = same & (r > cc)
    head0 = _iota((1, LANES), 1) < RWKV_HEAD
    bd = (_iota((LANES, LANES), 0) // RWKV_HEAD) == (_iota((LANES, LANES), 1) // RWKV_HEAD)
    units = [(b, p) for b in range(nb) for p in range(npair)]

    def cols(p):
        return slice(p * LANES, (p + 1) * LANES)

    def rows(ci):
        return slice(ci * c, (ci + 1) * c)

    def blk(ci):
        return slice(ci * LANES, (ci + 1) * LANES)

    def stack_heads(x):
        x0 = jnp.where(head0, x, jnp.zeros_like(x))
        x1 = jnp.where(head0, jnp.zeros_like(x), x)
        return jnp.concatenate([piece[rows(ci)] for ci in range(nck) for piece in (x0, x1)], axis=0)

    def stack_dup(x):
        return jnp.concatenate([x[rows(ci)] for ci in range(nck) for _ in range(2)], axis=0)

    for g0 in range(0, len(units), GROUP):
        grp = units[g0:g0 + GROUP]
        lks = [stack_heads(kkd_ref[b, :, cols(p)]) for b, p in grp]
        lrs = [stack_heads(rt_ref[b, :, cols(p)]) for b, p in grp]
        rks = [stack_dup(kt_ref[b, :, cols(p)]) for b, p in grp]
        rbs = [stack_dup(bt_ref[b, :, cols(p)]) for b, p in grp]
        v2s = [stack_dup(v_ref[b, :, cols(p)]) for b, p in grp]
        a_kb = [jnp.where(strict, _dot_nt(lk, rb), 0.0) for lk, rb in zip(lks, rbs)]
        a_kk = [jnp.where(strict, _dot_nt(lk, rk), 0.0) for lk, rk in zip(lks, rks)]
        ns = _tri_inv_m1(a_kb, c)
        avs = [_dot(a, v2) for a, v2 in zip(a_kk, v2s)]
        for idx in range(len(grp)):
            u = g0 + idx
            nbf = ns[idx].astype(BF16)
            n_scr[u] = nbf
            uloc_scr[u] = avs[idx] + _dot(nbf, avs[idx])
            a_rk = jnp.where(incl, _dot_nt(lrs[idx], rks[idx]), 0.0)
            yloc_scr[u] = _dot(a_rk, v2s[idx])
            arb_scr[u] = jnp.where(incl, _dot_nt(lrs[idx], rbs[idx]), 0.0).astype(BF16)

    nu = len(units)
    states = [st_ref[u] for u in range(nu)]
    rs_parts = [[] for _ in units]
    ub_parts = [[] for _ in units]
    for ci in range(nck):
        sbs = [s.astype(BF16) for s in states]
        kss = [_dot_nt(jnp.concatenate([kkd_ref[b, rows(ci), cols(p)], rt_ref[b, rows(ci), cols(p)]], axis=0), sbs[u])
               for u, (b, p) in enumerate(units)]
        ks2 = [jnp.concatenate([ks[:c], ks[:c]], axis=0) for ks in kss]
        tks = [ks2[u] + _dot(n_scr[u, blk(ci), blk(ci)], ks2[u]) for u in range(nu)]
        for u in range(nu):
            u2 = uloc_scr[u, blk(ci), :] + tks[u]
            ub_parts[u].append(jnp.where(head0, u2[:c], u2[c:]).astype(BF16))
            rs_parts[u].append(kss[u][c:])
        states = [states[u] * pc_ref[b, ci, :, cols(p)]
                  + jnp.where(bd, _dot_tn(jnp.concatenate([v_ref[b, rows(ci), cols(p)], ub_parts[u][ci]], axis=0),
                                          jnp.concatenate([kh_ref[b, rows(ci), cols(p)],
                                                           -bh_ref[b, rows(ci), cols(p)]], axis=0)), 0.0)
                  for u, (b, p) in enumerate(units)]

    for u, (b, p) in enumerate(units):
        u2all = jnp.concatenate([ub_parts[u][ci] for ci in range(nck) for _ in range(2)], axis=0)
        yall = yloc_scr[u] - _dot(arb_scr[u], u2all)
        for ci in range(nck):
            yb = yall[blk(ci), :]
            y_ref[b, rows(ci), cols(p)] = rs_parts[u][ci] + jnp.where(head0, yb[:c], yb[c:])
        st_ref[u] = states[u]


def rwkv_chunk(tensors, pc, nb, seq):
    t, w = tensors[0].shape
    tb = min(RWKV_STEP, seq)
    nck = tb // CHUNK
    nu = nb * (w // LANES)
    xspec = pl.BlockSpec((nb, tb, w), lambda s: (0, s, 0))
    out = pl.pallas_call(
        _rwkv_chunk_kernel,
        grid=(seq // tb,),
        in_specs=[xspec] * 7 + [pl.BlockSpec((nb, nck, 1, w), lambda s: (0, s, 0, 0))],
        out_specs=xspec,
        out_shape=jax.ShapeDtypeStruct((nb, seq, w), F32),
        scratch_shapes=[pltpu.VMEM((nu, LANES, LANES), F32),
                        pltpu.VMEM((nu, 2 * tb, 2 * tb), BF16),
                        pltpu.VMEM((nu, 2 * tb, 2 * tb), BF16),
                        pltpu.VMEM((nu, 2 * tb, LANES), F32),
                        pltpu.VMEM((nu, 2 * tb, LANES), F32)],
        compiler_params=_cparams("arbitrary"),
        name="rwkv_chunk",
    )(*[x.reshape(nb, seq, w) for x in tensors], pc.reshape(nb, seq // CHUNK, 1, w))
    return out.reshape(t, w)


def _rwkv_post_kernel(y_ref, bonus_ref, g_ref, lw_ref, lb_ref, o_ref):
    hm = _head_sum_matrix() * (1.0 / RWKV_HEAD)
    hm = hm.astype(BF16)
    for blk in range(RWKV_WIDTH // LANES):
        cols = slice(blk * LANES, (blk + 1) * LANES)
        y = y_ref[:, cols]
        mean = _dot_exact_rhs(y, hm, 3)
        yc = y - mean
        var = _dot_exact_rhs(yc * yc, hm, 3)
        yn = yc * lax.rsqrt(var + RWKV_LN_EPS) * lw_ref[:, cols] + lb_ref[:, cols]
        out = (yn + bonus_ref[:, cols].astype(F32)) * g_ref[:, cols].astype(F32)
        o_ref[:, cols] = out.astype(o_ref.dtype)


def rwkv_post(y, bonus, g, lnx_w, lnx_b, layer, tm):
    t, w = y.shape
    tm = min(tm, t)
    xspec = pl.BlockSpec((tm, w), lambda i: (i, 0))
    rspec = pl.BlockSpec((None, 1, w), lambda i: (layer, 0, 0))
    return pl.pallas_call(
        _rwkv_post_kernel,
        grid=(t // tm,),
        in_specs=[xspec, xspec, xspec, rspec, rspec],
        out_specs=xspec,
        out_shape=jax.ShapeDtypeStruct((t, w), BF16),
        compiler_params=_cparams("arbitrary"),
        name="rwkv_post",
    )(y, bonus, g, lnx_w[:, None, :], lnx_b[:, None, :])


def _relayout_w_in(w_in):
    nl, d, _ = w_in.shape

    def cols(lo, width):
        return w_in[:, :, lo:lo + width]

    def zeros(width):
        return jnp.zeros((nl, d, width), w_in.dtype)

    kr = cols(_O_KR, MLA_ROPE)
    half = MLA_ROPE // 2
    kr_swap = jnp.concatenate([-kr[:, :, half:], kr[:, :, :half]], axis=-1)
    w1 = jnp.concatenate([
        cols(_O_CQ, MLA_Q_RANK),
        kr, zeros(LANES - MLA_ROPE), kr_swap, zeros(LANES - MLA_ROPE),
        cols(_O_CKV, MLA_KV_RANK), zeros(GQ_OFF - CKV_OFF - MLA_KV_RANK),
        cols(_O_GDN, 4096),
        cols(_O_RWKV, 3072),
        cols(_O_GATE, 3 * D_MODEL)], axis=-1).astype(BF16)
    w2 = jnp.concatenate([
        cols(_O_XW, 256), cols(_O_GB, 16), zeros(NP2 - 272)], axis=-1).astype(BF16)
    return w1, w2


def _relayout_w_uq(w_uq):
    nl, r, _ = w_uq.shape
    wq = w_uq.reshape(nl, r, MLA_HEADS, MLA_NOPE + MLA_ROPE)
    nope = wq[..., :MLA_NOPE]
    rope = wq[..., MLA_NOPE:]
    half = MLA_ROPE // 2
    swap = jnp.concatenate([-rope[..., half:], rope[..., :half]], axis=-1)
    z = jnp.zeros(rope.shape[:-1] + (LANES - MLA_ROPE,), w_uq.dtype)
    return jnp.concatenate([nope, rope, z, swap, z], axis=-1).reshape(nl, r, 3 * MLA_HEADS * LANES).astype(BF16)


def _relayout_w_ukv(w_ukv):
    nl, r, _ = w_ukv.shape
    wkv = w_ukv.reshape(nl, r, MLA_HEADS, MLA_NOPE + MLA_V)
    wk = wkv[..., :MLA_NOPE].reshape(nl, r, MLA_HEADS * MLA_NOPE)
    wv = wkv[..., MLA_NOPE:].reshape(nl, r, MLA_HEADS * MLA_V)
    return jnp.concatenate([wk, wv], axis=-1).astype(BF16)


def _rope_table(positions):
    inv = 1.0 / (ROPE_THETA ** (jnp.arange(0, MLA_ROPE, 2, dtype=F32) / MLA_ROPE))
    ang = positions.astype(F32)[..., None] * inv
    cos, sin = jnp.cos(ang), jnp.sin(ang)
    z = jnp.zeros(cos.shape[:-1] + (LANES - MLA_ROPE,), F32)
    cs = jnp.concatenate([cos, cos, z, sin, sin, z], axis=-1)
    return cs.reshape(-1, 2 * LANES)


def kernel(x, c, positions, w_ada, b_ada, norm1_w, w_in, mla_q_norm_w, mla_w_uq, mla_kv_norm_w, mla_w_ukv,
           gdn_conv_w, gdn_a_log, gdn_dt_bias, gdn_norm_w, rwkv_mu, rwkv_w0, rwkv_w_up, rwkv_a0, rwkv_a_up,
           rwkv_g_up, rwkv_k_k, rwkv_k_a, rwkv_r_k, rwkv_lnx_w, rwkv_lnx_b, w_branch, w_out, norm2_w,
           w_gate_up, w_down, final_norm_w):
    nb, seq, d = x.shape
    nl = w_in.shape[0]
    assert seq % (2 * CHUNK) == 0 and d == D_MODEL

    w1, w2 = _relayout_w_in(w_in)
    wq3 = _relayout_w_uq(mla_w_uq)
    wkv = _relayout_w_ukv(mla_w_ukv)
    w_branch_b = w_branch.astype(BF16)
    w_out_b = w_out.astype(BF16)
    w_gu_b = w_gate_up.astype(BF16)
    w_down_b = w_down.astype(BF16)
    cs = _rope_table(positions)

    mod = adaln_mod(c, w_ada, b_ada)
    modr = mod.reshape(nl * nb * 6, 1, d)
    n1 = norm1_w[:, None, :]
    n2 = norm2_w[:, None, :]
    qnw = mla_q_norm_w[:, None, :]
    kvnw = mla_kv_norm_w[:, None, :]
    gnw = gdn_norm_w[:, None, :]

    xf = x.reshape(nb * seq, d)
    for l in range(nl):
        p1 = norm_gemm(xf, n1, modr, w1, l, 0, nb, seq, BF16, 1024, 512)
        p2 = norm_gemm(xf, n1, modr, w2, l, 0, nb, seq, F32, 1024, NP2)
        q, k, vt = mla_proj(p1, cs, qnw, kvnw, wq3, wkv, l, nb, seq, 512)
        o_a = flash_attn(q, k, vt, nb, seq, 512)
        qn, kn, vc, bg = gdn_prep(p1, p2, gdn_conv_w, gdn_a_log, gdn_dt_bias, l, seq, 512)
        o_b = gdn_chunk(qn, kn, vc, p1, bg, gnw, l, nb, seq)
        prep = rwkv_prep(p1, p2, rwkv_mu, rwkv_w0, rwkv_w_up, rwkv_a0, rwkv_a_up, rwkv_g_up,
                         rwkv_k_k, rwkv_k_a, rwkv_r_k, l, seq, 512)
        y = rwkv_chunk(prep[:7], prep[7], nb, seq)
        o_c = rwkv_post(y, prep[8], prep[9], rwkv_lnx_w, rwkv_lnx_b, l, 512)
        merged = merge_gemm(o_a, o_b, o_c, w_branch_b, p1, l, 1024, 512)
        xf = resid_gemm(merged, w_out_b, xf, modr, l, 2, nb, seq, 1024, 512)
        act = ffn_up(xf, n2, modr, w_gu_b, l, nb, seq, 1024, 512)
        xf = resid_gemm(act, w_down_b, xf, modr, l, 5, nb, seq, 1024, 512)
    return final_norm(xf, final_norm_w, 512).reshape(nb, seq, d)
```

```python
import functools
import math

import jax
import jax.numpy as jnp
from jax import lax
from jax.experimental import pallas as pl
from jax.experimental.pallas import tpu as pltpu

F32 = jnp.float32
BF16 = jnp.bfloat16

D_MODEL = 2048
MLA_HEADS = 8
MLA_Q_RANK = 768
MLA_KV_RANK = 512
MLA_NOPE = 128
MLA_ROPE = 64
MLA_V = 128
ROPE_THETA = 10000.0
GDN_HEADS = 8
GDN_DK = 128
GDN_DV = 128
GDN_CONV = 4
RWKV_HEADS = 16
RWKV_HEAD = 64
RWKV_WIDTH = RWKV_HEADS * RWKV_HEAD
RWKV_W_RANK = 64
RWKV_A_RANK = 64
RWKV_G_RANK = 128
RWKV_LN_EPS = 64e-5
D_FF = 5632
NORM_EPS = 1e-6
CHUNK = 64
MXU_TILE = 256
GDN_STEP = MXU_TILE
RWKV_STEP = MXU_TILE // 2
GROUP = 4
FLASH_HEADS = 8
VT_PAD = 16
LANES = 128
SUBLANES = 8
VMEM_LIMIT = 56 * 1024 * 1024

CQ_OFF = 0
KR_OFF = 768
CKV_OFF = 1024
GQ_OFF = 2048
RR_OFF = 6144
GATE_OFF = 9216
NP1 = 15360
NP2 = 384

GEMM_TILES = {
    "w_in": (1024, 1024),
    "w_in_f32": (1024, NP2),
    "merge": (1024, 1024),
    "w_out": (1024, 1024),
    "ffn_up": (1024, 512),
    "ffn_down": (1024, 512),
}
ROW_TILE = 512
ATTN_TILE = 512

_O_CQ = 0
_O_CKV = 768
_O_KR = 1280
_O_GDN = 1344
_O_GZ = _O_GDN + 3072
_O_GB = _O_GZ + 1024
_O_GA = _O_GB + 8
_O_RWKV = _O_GA + 8
_O_XW = _O_RWKV + 3072
_O_XA = _O_XW + 64
_O_XG = _O_XA + 64
_O_GATE = _O_XG + 128


def _cparams(*sem):
    return pltpu.CompilerParams(dimension_semantics=sem, vmem_limit_bytes=VMEM_LIMIT)


def _dot(a, b):
    return jnp.dot(a.astype(BF16), b.astype(BF16), preferred_element_type=F32)


def _dot_nt(a, b):
    return lax.dot_general(a.astype(BF16), b.astype(BF16), (((1,), (1,)), ((), ())),
                           preferred_element_type=F32)


def _dot_tn(a, b):
    return lax.dot_general(a.astype(BF16), b.astype(BF16), (((0,), (0,)), ((), ())),
                           preferred_element_type=F32)


def _split(a, terms):
    parts = []
    rem = a
    for _ in range(terms):
        p = rem.astype(BF16)
        parts.append(p)
        rem = rem - p.astype(F32)
    return parts


def _dot_exact_lhs(a_bf, b, terms):
    out = None
    for p in _split(b, terms):
        t = _dot(a_bf, p)
        out = t if out is None else out + t
    return out


def _dot_exact_rhs(a, b_bf, terms):
    out = None
    for p in _split(a, terms):
        t = _dot(p, b_bf)
        out = t if out is None else out + t
    return out


def _iota(shape, dim):
    return lax.broadcasted_iota(jnp.int32, shape, dim)


def _tri_inv_m1(a_list, n):
    ns = [-a for a in a_list]
    ps = [_dot(a, a) for a in a_list]
    k = 2
    while True:
        pbs = [p.astype(BF16) for p in ps]
        ns = [nn + p + _dot(nn, pb) for nn, p, pb in zip(ns, ps, pbs)]
        k *= 2
        if k >= n:
            break
        ps = [_dot(pb, pb) for pb in pbs]
    return ns


def _sigmoid(x):
    return jax.nn.sigmoid(x)


def _softplus(x):
    return jnp.maximum(x, 0.0) + jnp.log1p(jnp.exp(-jnp.abs(x)))


def _mod_kernel(c_ref, w_ref, b_ref, o_ref):
    w = w_ref[...]
    for m in range(c_ref.shape[0]):
        c = c_ref[m]
        ca = c * _sigmoid(c)
        o_ref[m:m + 1, :] = jnp.sum(ca * w, axis=0, keepdims=True) + b_ref[...]


def adaln_mod(c, w_ada, b_ada):
    nl, d, n = w_ada.shape
    b = c.shape[0]
    tn = min(1024, n)
    return pl.pallas_call(
        _mod_kernel,
        grid=(nl, n // tn),
        in_specs=[pl.BlockSpec((b, d, 1), lambda l, j: (0, 0, 0)),
                  pl.BlockSpec((None, d, tn), lambda l, j: (l, 0, j)),
                  pl.BlockSpec((None, 1, tn), lambda l, j: (l, 0, j))],
        out_specs=pl.BlockSpec((None, b, tn), lambda l, j: (l, 0, j)),
        out_shape=jax.ShapeDtypeStruct((nl, b, n), F32),
        compiler_params=_cparams("arbitrary", "arbitrary"),
        name="adaln_mod",
    )(c[:, :, None], w_ada, b_ada[:, None, :])


def _modulate_to(h_scr, x_ref, nw_ref, sh_ref, sc_ref):
    x = x_ref[...]
    ms = jnp.mean(x * x, axis=-1, keepdims=True)
    y = x * lax.rsqrt(ms + NORM_EPS) * nw_ref[...]
    h_scr[...] = (y * (1.0 + sc_ref[...]) + sh_ref[...]).astype(BF16)


def _norm_gemm_kernel(x_ref, nw_ref, sh_ref, sc_ref, w_ref, o_ref, h_scr):
    @pl.when(pl.program_id(1) == 0)
    def _():
        _modulate_to(h_scr, x_ref, nw_ref, sh_ref, sc_ref)

    o_ref[...] = lax.dot_general(h_scr[...], w_ref[...], (((1,), (1,)), ((), ())),
                                 preferred_element_type=F32).astype(o_ref.dtype)


def _mod_spec(d, layer, nb, which, tiles_per_seq):
    return pl.BlockSpec((None, 1, d),
                        lambda i, j: ((layer * nb + i // tiles_per_seq) * 6 + which, 0, 0))


def norm_gemm(x, norm_w, modr, w, layer, which_sh, nb, seq, out_dtype, tm, tn):
    t, d = x.shape
    n = w.shape[1]
    tm = min(tm, seq)
    tn = min(tn, n)
    tps = seq // tm
    return pl.pallas_call(
        _norm_gemm_kernel,
        grid=(t // tm, n // tn),
        in_specs=[pl.BlockSpec((tm, d), lambda i, j: (i, 0)),
                  pl.BlockSpec((None, 1, d), lambda i, j: (layer, 0, 0)),
                  _mod_spec(d, layer, nb, which_sh, tps),
                  _mod_spec(d, layer, nb, which_sh + 1, tps),
                  pl.BlockSpec((None, tn, d), lambda i, j: (layer, j, 0))],
        out_specs=pl.BlockSpec((tm, tn), lambda i, j: (i, j)),
        out_shape=jax.ShapeDtypeStruct((t, n), out_dtype),
        scratch_shapes=[pltpu.VMEM((tm, d), BF16)],
        compiler_params=_cparams("arbitrary", "arbitrary"),
        name="norm_gemm",
    )(x, norm_w, modr, modr, w)


def _ffn_up_kernel(x_ref, nw_ref, sh_ref, sc_ref, wg_ref, wu_ref, o_ref, h_scr):
    @pl.when(pl.program_id(1) == 0)
    def _():
        _modulate_to(h_scr, x_ref, nw_ref, sh_ref, sc_ref)

    h = h_scr[...]
    gate = jnp.dot(h, wg_ref[...], preferred_element_type=F32)
    up = jnp.dot(h, wu_ref[...], preferred_element_type=F32)
    o_ref[...] = (gate * _sigmoid(gate) * up).astype(o_ref.dtype)


def ffn_up(x, norm_w, modr, w_gu, layer, nb, seq, tm, tn):
    t, d = x.shape
    f = w_gu.shape[-1] // 2
    tm = min(tm, seq)
    tn = min(tn, f)
    tps = seq // tm
    nf = f // tn
    return pl.pallas_call(
        _ffn_up_kernel,
        grid=(t // tm, nf),
        in_specs=[pl.BlockSpec((tm, d), lambda i, j: (i, 0)),
                  pl.BlockSpec((None, 1, d), lambda i, j: (layer, 0, 0)),
                  _mod_spec(d, layer, nb, 3, tps),
                  _mod_spec(d, layer, nb, 4, tps),
                  pl.BlockSpec((None, d, tn), lambda i, j: (layer, 0, j)),
                  pl.BlockSpec((None, d, tn), lambda i, j: (layer, 0, j + nf))],
        out_specs=pl.BlockSpec((tm, tn), lambda i, j: (i, j)),
        out_shape=jax.ShapeDtypeStruct((t, f), BF16),
        scratch_shapes=[pltpu.VMEM((tm, d), BF16)],
        compiler_params=_cparams("arbitrary", "arbitrary"),
        name="ffn_up",
    )(x, norm_w, modr, modr, w_gu, w_gu)


def _resid_gemm_kernel(a_ref, w_ref, x_ref, gt_ref, o_ref):
    y = jnp.dot(a_ref[...], w_ref[...], preferred_element_type=F32)
    o_ref[...] = x_ref[...] + gt_ref[...] * y


def resid_gemm(a, w, x, modr, layer, which_gt, nb, seq, tm, tn):
    t, k = a.shape
    d = x.shape[-1]
    tm = min(tm, seq)
    tn = min(tn, d)
    tps = seq // tm
    return pl.pallas_call(
        _resid_gemm_kernel,
        grid=(t // tm, d // tn),
        in_specs=[pl.BlockSpec((tm, k), lambda i, j: (i, 0)),
                  pl.BlockSpec((None, k, tn), lambda i, j: (layer, 0, j)),
                  pl.BlockSpec((tm, tn), lambda i, j: (i, j)),
                  pl.BlockSpec((None, 1, tn),
                               lambda i, j: ((layer * nb + i // tps) * 6 + which_gt, 0, j))],
        out_specs=pl.BlockSpec((tm, tn), lambda i, j: (i, j)),
        out_shape=jax.ShapeDtypeStruct((t, d), F32),
        compiler_params=_cparams("arbitrary", "arbitrary"),
        name="resid_gemm",
    )(a, w, x, modr)


def _merge_kernel(oa_ref, ob_ref, oc_ref, wa_ref, wb_ref, wc_ref, ga_ref, gb_ref, gc_ref, o_ref):
    def branch(o_r, w_r, g_r):
        y = jnp.dot(o_r[...], w_r[...], preferred_element_type=F32)
        return _sigmoid(g_r[...].astype(F32)) * y

    acc = branch(oa_ref, wa_ref, ga_ref) + branch(ob_ref, wb_ref, gb_ref) + branch(oc_ref, wc_ref, gc_ref)
    o_ref[...] = acc.astype(o_ref.dtype)


def merge_gemm(o_a, o_b, o_c, w_branch, p1, layer, tm, tn):
    t, kb = o_a.shape
    d = w_branch.shape[-1]
    tm = min(tm, t)
    tn = min(tn, d)
    gblk = GATE_OFF // tn
    dblk = d // tn

    def ospec():
        return pl.BlockSpec((tm, kb), lambda i, j: (i, 0))

    def wspec(r):
        return pl.BlockSpec((None, kb, tn), lambda i, j: (layer, r, j))

    def gspec(r):
        return pl.BlockSpec((tm, tn), lambda i, j: (i, gblk + r * dblk + j))

    return pl.pallas_call(
        _merge_kernel,
        grid=(t // tm, d // tn),
        in_specs=[ospec(), ospec(), ospec(), wspec(0), wspec(1), wspec(2), gspec(0), gspec(1), gspec(2)],
        out_specs=pl.BlockSpec((tm, tn), lambda i, j: (i, j)),
        out_shape=jax.ShapeDtypeStruct((t, d), BF16),
        compiler_params=_cparams("arbitrary", "arbitrary"),
        name="merge_gemm",
    )(o_a, o_b, o_c, w_branch, w_branch, w_branch, p1, p1, p1)


def _final_norm_kernel(x_ref, w_ref, o_ref):
    x = x_ref[...]
    ms = jnp.mean(x * x, axis=-1, keepdims=True)
    o_ref[...] = x * lax.rsqrt(ms + NORM_EPS) * w_ref[...]


def final_norm(x, w, tm):
    t, d = x.shape
    tm = min(tm, t)
    return pl.pallas_call(
        _final_norm_kernel,
        grid=(t // tm,),
        in_specs=[pl.BlockSpec((tm, d), lambda i: (i, 0)), pl.BlockSpec((1, d), lambda i: (0, 0))],
        out_specs=pl.BlockSpec((tm, d), lambda i: (i, 0)),
        out_shape=jax.ShapeDtypeStruct((t, d), F32),
        compiler_params=_cparams("arbitrary"),
        name="final_norm",
    )(x, w[None, :])


def _rms_rows(x, w):
    ms = jnp.mean(x * x, axis=-1, keepdims=True)
    return x * lax.rsqrt(ms + NORM_EPS) * w


def _mla_proj_kernel(cq_ref, kr_ref, ckv_ref, cs_ref, qnw_ref, kvnw_ref, wq_ref, wkv_ref,
                     q_ref, k_ref, vt_ref):
    scale =(MLA_NOPE + MLA_ROPE) ** -0.5 * math.log2(math.e)
    cc = cs_ref[:, 0:LANES]
    ss = cs_ref[:, LANES:2 * LANES]
    nq = _rms_rows(cq_ref[...].astype(F32), qnw_ref[...]).astype(BF16)
    q3 = jnp.dot(nq, wq_ref[...], preferred_element_type=F32)
    for h in range(MLA_HEADS):
        b0 = h * 3 * LANES
        nope = q3[:, b0:b0 + LANES]
        rope = q3[:, b0 + LANES:b0 + 2 * LANES] * cc + q3[:, b0 + 2 * LANES:b0 + 3 * LANES] * ss
        q_ref[:, 2 * h * LANES:(2 * h + 1) * LANES] = (nope * scale).astype(BF16)
        q_ref[:, (2 * h + 1) * LANES:(2 * h + 2) * LANES] = (rope * scale).astype(BF16)
    nkv = _rms_rows(ckv_ref[...].astype(F32), kvnw_ref[...]).astype(BF16)
    kv = jnp.dot(nkv, wkv_ref[...], preferred_element_type=F32)
    kr = kr_ref[...].astype(F32)
    krope = (kr[:, 0:LANES] * cc + kr[:, LANES:2 * LANES] * ss).astype(BF16)
    for h in range(MLA_HEADS):
        k_ref[:, 2 * h * LANES:(2 * h + 1) * LANES] = kv[:, h * LANES:(h + 1) * LANES].astype(BF16)
        k_ref[:, (2 * h + 1) * LANES:(2 * h + 2) * LANES] = krope
    ones = jnp.ones((VT_PAD, kv.shape[0]), BF16)
    for h in range(MLA_HEADS):
        vh = kv[:, (MLA_HEADS + h) * LANES:(MLA_HEADS + h + 1) * LANES]
        vt_ref[h, 0:MLA_V, :] = vh.T.astype(BF16)
        vt_ref[h, MLA_V:MLA_V + VT_PAD, :] = ones


def mla_proj(p1, cs, q_norm_w, kv_norm_w, wq3, wkv, layer, nb, seq, tm):
    t = p1.shape[0]
    tm = min(tm, seq)
    tps = seq // tm
    hq = MLA_HEADS * 2 * LANES
    return pl.pallas_call(
        _mla_proj_kernel,
        grid=(t // tm,),
        in_specs=[pl.BlockSpec((tm, MLA_Q_RANK), lambda i: (i, CQ_OFF // MLA_Q_RANK)),
                  pl.BlockSpec((tm, 2 * LANES), lambda i: (i, KR_OFF // (2 * LANES))),
                  pl.BlockSpec((tm, MLA_KV_RANK), lambda i: (i, CKV_OFF // MLA_KV_RANK)),
                  pl.BlockSpec((tm, 2 * LANES), lambda i: (i, 0)),
                  pl.BlockSpec((None, 1, MLA_Q_RANK), lambda i: (layer, 0, 0)),
                  pl.BlockSpec((None, 1, MLA_KV_RANK), lambda i: (layer, 0, 0)),
                  pl.BlockSpec((None, MLA_Q_RANK, 3 * MLA_HEADS * LANES), lambda i: (layer, 0, 0)),
                  pl.BlockSpec((None, MLA_KV_RANK, 2 * MLA_HEADS * LANES), lambda i: (layer, 0, 0))],
        out_specs=[pl.BlockSpec((tm, hq), lambda i: (i, 0)),
                   pl.BlockSpec((tm, hq), lambda i: (i, 0)),
                   pl.BlockSpec((None, MLA_HEADS, MLA_V + VT_PAD, tm), lambda i: (i // tps, 0, 0, i % tps))],
        out_shape=[jax.ShapeDtypeStruct((t, hq), BF16),
                   jax.ShapeDtypeStruct((t, hq), BF16),
                   jax.ShapeDtypeStruct((nb, MLA_HEADS, MLA_V + VT_PAD, seq), BF16)],
        compiler_params=_cparams("arbitrary"),
        name="mla_proj",
    )(p1, p1, p1, cs, q_norm_w, kv_norm_w, wq3, wkv)


def _flash_kernel(it_ref, jt_ref, q_ref, k_ref, vt_ref, o_ref, m_scr, acc_scr):
    t = pl.program_id(2)
    i = it_ref[t]
    j = jt_ref[t]
    heads = range(FLASH_HEADS)

    @pl.when(j == 0)
    def _():
        m_scr[...] = jnp.full(m_scr.shape, -jnp.inf, F32)
        acc_scr[...] = jnp.zeros(acc_scr.shape, F32)

    def step(masked):
        sts = [lax.dot_general(k_ref[:, 2 * g * LANES:2 * (g + 1) * LANES],
                               q_ref[:, 2 * g * LANES:2 * (g + 1) * LANES],
                               (((1,), (1,)), ((), ())), preferred_element_type=F32) for g in heads]
        if masked:
            keep = _iota(sts[0].shape, 1) >= _iota(sts[0].shape, 0)
            sts = [jnp.where(keep, s, -jnp.inf) for s in sts]
        m_prev = [m_scr[g] for g in heads]
        m_new = [jnp.maximum(mp, jnp.max(s, axis=0, keepdims=True)) for mp, s in zip(m_prev, sts)]
        pts = [jnp.exp2(s - mn).astype(BF16) for s, mn in zip(sts, m_new)]
        alphas = [jnp.exp2(mp - mn) for mp, mn in zip(m_prev, m_new)]
        pvs = [jnp.dot(vt_ref[g], pt, preferred_element_type=F32) for g, pt in zip(heads, pts)]
        for g in heads:
            acc_scr[g] = alphas[g] * acc_scr[g] + pvs[g]
            m_scr[g] = m_new[g]

    @pl.when(j < i)
    def _():
        step(False)

    @pl.when(j == i)
    def _():
        step(True)
        for g in heads:
            acc = acc_scr[g]
            o_t = acc[0:MLA_V, :] / acc[MLA_V:MLA_V + 1, :]
            o_ref[:, g * LANES:(g + 1) * LANES] = o_t.T.astype(o_ref.dtype)


def flash_attn(q, k, vt, nb, seq, tq):
    tq = min(tq, seq)
    nq = seq // tq
    hg = FLASH_HEADS
    pairs = [(i, j) for i in range(nq) for j in range(i + 1)]
    it = jnp.asarray([p[0] for p in pairs], jnp.int32)
    jt = jnp.asarray([p[1] for p in pairs], jnp.int32)
    grid_spec = pltpu.PrefetchScalarGridSpec(
        num_scalar_prefetch=2,
        grid=(nb, MLA_HEADS // hg, len(pairs)),
        in_specs=[pl.BlockSpec((tq, hg * 2 * LANES), lambda b, h, t, it_r, jt_r: (b * nq + it_r[t], h)),
                  pl.BlockSpec((tq, hg * 2 * LANES), lambda b, h, t, it_r, jt_r: (b * nq + jt_r[t], h)),
                  pl.BlockSpec((None, hg, MLA_V + VT_PAD, tq), lambda b, h, t, it_r, jt_r: (b, h, 0, jt_r[t]))],
        out_specs=pl.BlockSpec((tq, hg * LANES), lambda b, h, t, it_r, jt_r: (b * nq + it_r[t], h)),
        scratch_shapes=[pltpu.VMEM((hg, 1, tq), F32), pltpu.VMEM((hg, MLA_V + VT_PAD, tq), F32)],
    )
    return pl.pallas_call(
        _flash_kernel,
        grid_spec=grid_spec,
        out_shape=jax.ShapeDtypeStruct((nb * seq, MLA_HEADS * LANES), BF16),
        compiler_params=_cparams("arbitrary", "arbitrary", "arbitrary"),
        name="flash_attn",
    )(it, jt, q, k, vt)


def _shifted_rows(x, halo, d):
    xr = pltpu.roll(x, d, 0)
    hr = pltpu.roll(halo, d, 0)
    top = jnp.where(_iota(halo.shape, 0) < d, hr, xr[0:SUBLANES])
    return jnp.concatenate([top, xr[SUBLANES:]], axis=0)


def _gdn_prep_kernel(q_ref, k_ref, v_ref, qh_ref, kh_ref, vh_ref, cwq_ref, cwk_ref, cwv_ref,
                     p2_ref, alog_ref, dtb_ref, qo_ref, ko_ref, vo_ref, bg_ref, *, tiles_per_seq):
    first = (pl.program_id(0) % tiles_per_seq) == 0

    def conv_silu(x_ref, h_ref, w_ref):
        x = x_ref[...].astype(F32)
        halo = jnp.where(first, 0.0, h_ref[...].astype(F32))
        w = w_ref[...]
        acc = x * w[GDN_CONV - 1:GDN_CONV, :]
        for d in range(1, GDN_CONV):
            acc = acc + _shifted_rows(x, halo, d) * w[GDN_CONV - 1 - d:GDN_CONV - d, :]
        return acc * _sigmoid(acc)

    def l2n(x, mult):
        outs = []
        for h in range(GDN_HEADS):
            xh = x[:, h * LANES:(h + 1) * LANES]
            ss = jnp.sum(xh * xh, axis=-1, keepdims=True)
            outs.append(xh * (lax.rsqrt(ss + 1e-6) * mult))
        return jnp.concatenate(outs, axis=1)

    qo_ref[...] = l2n(conv_silu(q_ref, qh_ref, cwq_ref), GDN_DK ** -0.5).astype(BF16)
    ko_ref[...] = l2n(conv_silu(k_ref, kh_ref, cwk_ref), 1.0).astype(BF16)
    vo_ref[...] = conv_silu(v_ref, vh_ref, cwv_ref).astype(BF16)
    p2 = p2_ref[...]
    lane = _iota(p2.shape, 1)
    g = -jnp.exp(alog_ref[...]) * _softplus(p2 + dtb_ref[...])
    bg_ref[...] = jnp.where(lane < GDN_HEADS, _sigmoid(p2), g)


def gdn_prep(p1, p2, conv_w, a_log, dt_bias, layer, seq, tm):
    t = p1.shape[0]
    tm = min(tm, seq)
    w = GDN_HEADS * GDN_DK
    cblk = GQ_OFF // w
    hb = tm // SUBLANES

    def xspec(r):
        return pl.BlockSpec((tm, w), lambda i: (i, cblk + r))

    def hspec(r):
        return pl.BlockSpec((SUBLANES, w), lambda i: (jnp.maximum(i * hb - 1, 0), cblk + r))

    def cwspec(r):
        return pl.BlockSpec((None, GDN_CONV, w), lambda i: (layer, 0, r))

    pad = jnp.zeros((LANES - 2 * GDN_HEADS,), F32)
    z8 = jnp.zeros((GDN_HEADS,), F32)
    alog_row = jnp.concatenate([z8, a_log[layer], pad])[None, :]
    dtb_row = jnp.concatenate([z8, dt_bias[layer], pad])[None, :]
    return pl.pallas_call(
        functools.partial(_gdn_prep_kernel, tiles_per_seq=seq // tm),
        grid=(t // tm,),
        in_specs=[xspec(0), xspec(1), xspec(2), hspec(0), hspec(1), hspec(2),
                  cwspec(0), cwspec(1), cwspec(2),
                  pl.BlockSpec((tm, LANES), lambda i: (i, 2)),
                  pl.BlockSpec((1, LANES), lambda i: (0, 0)),
                  pl.BlockSpec((1, LANES), lambda i: (0, 0))],
        out_specs=[pl.BlockSpec((tm, w), lambda i: (i, 0))] * 3 + [pl.BlockSpec((tm, LANES), lambda i: (i, 0))],
        out_shape=[jax.ShapeDtypeStruct((t, w), BF16)] * 3 + [jax.ShapeDtypeStruct((t, LANES), F32)],
        compiler_params=_cparams("arbitrary"),
        name="gdn_prep",
    )(p1, p1, p1, p1, p1, p1, conv_w, conv_w, conv_w, p2, alog_row, dtb_row)


def _gdn_chunk_kernel(q_ref, k_ref, v_ref, z_ref, bg_ref, nw_ref, o_ref, st_ref, uw_scr, qk_scr, qg_scr, kg_scr):
    nb, tb, _ = q_ref.shape
    c = CHUNK
    nck = tb // c

    @pl.when(pl.program_id(0) == 0)
    def _():
        st_ref[...] = jnp.zeros(st_ref.shape, F32)

    r = _iota((tb, tb), 0)
    cc = _iota((tb, tb), 1)
    same = (r // c) == (cc // c)
    incl = same & (r >= cc)
    strict = same & (r > cc)
    units = [(b, h) for b in range(nb) for h in range(GDN_HEADS)]
    lmat = jnp.where(incl, 1.0, 0.0).astype(BF16)
    g_cum = [_dot_exact_lhs(lmat, bg_ref[b], 3) for b in range(nb)]
    g_tot = [jnp.concatenate([jnp.broadcast_to(g[(ci + 1) * c - 1:(ci + 1) * c, :], (c, LANES))
                              for ci in range(nck)], axis=0) for g in g_cum]
    g_cum_t = [g.T for g in g_cum]

    def cols(h):
        return slice(h * LANES, (h + 1) * LANES)

    def glane(h):
        return slice(GDN_HEADS + h, GDN_HEADS + h + 1)

    for g0 in range(0, len(units), GROUP):
        grp = units[g0:g0 + GROUP]
        ks = [k_ref[b, :, cols(h)] for b, h in grp]
        gcs = [g_cum[b][:, glane(h)] for b, h in grp]
        betas = [bg_ref[b, :, h:h + 1] for b, h in grp]
        decs = [jnp.exp(jnp.where(incl, gc - g_cum_t[b][glane(h), :], -jnp.inf))
                for gc, (b, h) in zip(gcs, grp)]
        a_s = [jnp.where(strict, _dot_nt(k, k) * dec, 0.0) * beta for k, dec, beta in zip(ks, decs, betas)]
        ns = _tri_inv_m1(a_s, c)
        for idx, (b, h) in enumerate(grp):
            u = g0 + idx
            q = q_ref[b, :, cols(h)]
            kf = ks[idx].astype(F32)
            vf = v_ref[b, :, cols(h)].astype(F32)
            eg = jnp.exp(gcs[idx])
            rhs = jnp.concatenate([betas[idx] * vf, (betas[idx] * eg) * kf], axis=1)
            uw_scr[u] = rhs + _dot(ns[idx], rhs)
            qk_scr[u] = (_dot_nt(q, ks[idx]) * decs[idx]).astype(BF16)
            qg_scr[u] = (q.astype(F32) * eg).astype(BF16)
            kg_scr[u] = (kf * jnp.exp(g_tot[b][:, glane(h)] - gcs[idx])).astype(BF16)

    states = [st_ref[u] for u in range(len(units))]
    w_parts = [[] for _ in units]
    og_parts = [[] for _ in units]
    for ci in range(nck):
        rows = slice(ci * c, (ci + 1) * c)
        sbs = [s.astype(BF16) for s in states]
        wqs = [_dot(jnp.concatenate([uw_scr[u, rows, GDN_DV:].astype(BF16), qg_scr[u, rows, :]], axis=0), sbs[u])
               for u in range(len(units))]
        for u in range(len(units)):
            w_parts[u].append((uw_scr[u, rows, :GDN_DV] - wqs[u][:c]).astype(BF16))
            og_parts[u].append(wqs[u][c:])
        states = [jnp.exp(g_tot[b][ci * c:ci * c + 1, glane(h)]) * states[u]
                  + _dot_tn(kg_scr[u, rows, :], w_parts[u][ci])
                  for u, (b, h) in enumerate(units)]

    nw = nw_ref[...]
    for u, (b, h) in enumerate(units):
        o = jnp.concatenate(og_parts[u], axis=0) + _dot(qk_scr[u], jnp.concatenate(w_parts[u], axis=0))
        zf = z_ref[b, :, cols(h)].astype(F32)
        o_ref[b, :, cols(h)] = (_rms_rows(o, nw) * (zf * _sigmoid(zf))).astype(o_ref.dtype)
        st_ref[u] = states[u]


def gdn_chunk(qn, kn, vc, p1, bg, norm_w, layer, nb, seq):
    t, w = qn.shape
    tb = min(GDN_STEP, seq)
    zblk = (GQ_OFF + 3 * w) // w
    nu = nb * GDN_HEADS

    def xspec(cb):
        return pl.BlockSpec((nb, tb, w), lambda s: (0, s, cb))

    out = pl.pallas_call(
        _gdn_chunk_kernel,
        grid=(seq // tb,),
        in_specs=[xspec(0), xspec(0), xspec(0), xspec(zblk),
                  pl.BlockSpec((nb, tb, LANES), lambda s: (0, s, 0)),
                  pl.BlockSpec((None, 1, GDN_DV), lambda s: (layer, 0, 0))],
        out_specs=xspec(0),
        out_shape=jax.ShapeDtypeStruct((nb, seq, w), BF16),
        scratch_shapes=[pltpu.VMEM((nu, GDN_DK, GDN_DV), F32),
                        pltpu.VMEM((nu, tb, 2 * GDN_DV), F32),
                        pltpu.VMEM((nu, tb, tb), BF16),
                        pltpu.VMEM((nu, tb, GDN_DK), BF16),
                        pltpu.VMEM((nu, tb, GDN_DK), BF16)],
        compiler_params=_cparams("arbitrary"),
        name="gdn_chunk",
    )(qn.reshape(nb, seq, w), kn.reshape(nb, seq, w), vc.reshape(nb, seq, w),
      p1.reshape(nb, seq, p1.shape[-1]), bg.reshape(nb, seq, LANES), norm_w)
    return out.reshape(t, w)


def _head_sum_matrix():
    return ((_iota((LANES, LANES), 0) // RWKV_HEAD) == (_iota((LANES, LANES), 1) // RWKV_HEAD)).astype(BF16)


def _rwkv_prep_kernel(r_ref, k_ref, v_ref, x_ref, rh_ref, kh_ref, vh_ref, xh_ref,
                      mur_ref, muk_ref, muv_ref, mux_ref, w0_ref, wup_ref, a0_ref, aup_ref, gup_ref,
                      kk_w_ref, ka_ref, rk_ref,
                      kkd_ref, rt_ref, kt_ref, bt_ref, kh_o_ref, bh_o_ref, v_o_ref, pc_ref, bonus_ref, g_ref,
                      *, tiles_per_seq):
    first = (pl.program_id(0) % tiles_per_seq) == 0
    c = CHUNK
    tm = r_ref.shape[0]

    def shift_mix(x_r, h_r, mu_r):
        x = x_r[...].astype(F32)
        halo = jnp.where(first, 0.0, h_r[...].astype(F32))
        prev = _shifted_rows(x, halo, 1)
        return x + (prev - x) * mu_r[...]

    r = shift_mix(r_ref, rh_ref, mur_ref)
    k = shift_mix(k_ref, kh_ref, muk_ref)
    v = shift_mix(v_ref, vh_ref, muv_ref)
    xs = shift_mix(x_ref, xh_ref, mux_ref)
    xw = xs[:, 0:LANES]
    lane = _iota(xw.shape, 1)
    tw = jnp.where(lane < RWKV_W_RANK, jnp.tanh(xw), 0.0)
    xa = jnp.where(lane >= RWKV_W_RANK, xw, 0.0)
    sg = _sigmoid(xs[:, LANES:2 * LANES])
    w_log = -_softplus(-(w0_ref[...] + _dot(tw, wup_ref[...]))) - 0.5
    lw = -jnp.exp(w_log)
    a = _sigmoid(a0_ref[...] + _dot(xa, aup_ref[...]))
    g = _dot(sg, gup_ref[...])
    hs = _head_sum_matrix()
    kkw = k * kk_w_ref[...]
    k2 = k * (1.0 + (a - 1.0) * ka_ref[...])
    rkr = r * k2 * rk_ref[...]
    kk_parts = []
    bonus_parts = []
    for blk in range(RWKV_WIDTH // LANES):
        cols = slice(blk * LANES, (blk + 1) * LANES)
        x_blk = kkw[:, cols]
        ss = _dot_exact_rhs(x_blk * x_blk, hs, 2)
        kk_parts.append(x_blk * lax.rsqrt(ss + 1e-6))
        bonus_parts.append(_dot_exact_rhs(rkr[:, cols], hs, 2) * v[:, cols])
    kk = jnp.concatenate(kk_parts, axis=1)
    bonus_ref[...] = jnp.concatenate(bonus_parts, axis=1).astype(bonus_ref.dtype)
    g_ref[...] = g.astype(g_ref.dtype)
    b = kk * a
    ri = _iota((tm, tm), 0)
    cj = _iota((tm, tm), 1)
    same = (ri // c) == (cj // c)
    gi = _dot_exact_lhs(jnp.where(same & (ri >= cj), 1.0, 0.0).astype(BF16), lw, 3)
    ge = gi - lw
    e_neg = jnp.exp(-gi)
    kkd_ref[...] = (kk * jnp.exp(ge)).astype(BF16)
    rt_ref[...] = (r * jnp.exp(gi)).astype(BF16)
    kt_ref[...] = (k2 * e_neg).astype(BF16)
    bt_ref[...] = (b * e_neg).astype(BF16)
    v_o_ref[...] = v.astype(BF16)
    for ci in range(tm // c):
        rows = slice(ci * c, (ci + 1) * c)
        egl = jnp.exp(gi[(ci + 1) * c - 1:(ci + 1) * c, :])
        pc_ref[ci] = egl
        e_tail = egl * e_neg[rows]
        kh_o_ref[rows, :] = (k2[rows] * e_tail).astype(BF16)
        bh_o_ref[rows, :] = (b[rows] * e_tail).astype(BF16)


def rwkv_prep(p1, p2, mu, w0, w_up, a0, a_up, g_up, k_k, k_a, r_k, layer, seq, tm):
    t = p1.shape[0]
    tm = min(tm, seq)
    w = RWKV_WIDTH
    cblk = RR_OFF // w
    hb = tm // SUBLANES
    xw2 = 2 * LANES

    def xspec(r):
        return pl.BlockSpec((tm, w), lambda i: (i, cblk + r))

    def hspec(r):
        return pl.BlockSpec((SUBLANES, w), lambda i: (jnp.maximum(i * hb - 1, 0), cblk + r))

    def row(width, blk):
        return pl.BlockSpec((None, 1, width), lambda i: (layer, 0, blk))

    zpad = jnp.zeros((w_up.shape[0], RWKV_W_RANK, w), F32)
    wup_p = jnp.concatenate([w_up, zpad], axis=1).astype(BF16)
    aup_p = jnp.concatenate([zpad, a_up], axis=1).astype(BF16)
    gup_b = g_up.astype(BF16)
    mu3 = mu[:, None, :]
    rk_row = r_k.reshape(r_k.shape[0], 1, w)
    outs = [jax.ShapeDtypeStruct((t, w), BF16)] * 7
    outs += [jax.ShapeDtypeStruct((t // CHUNK, 1, w), F32),
             jax.ShapeDtypeStruct((t, w), BF16), jax.ShapeDtypeStruct((t, w), BF16)]
    ospec = pl.BlockSpec((tm, w), lambda i: (i, 0))
    out_specs = [ospec] * 7 + [pl.BlockSpec((tm // CHUNK, 1, w), lambda i: (i, 0, 0)), ospec, ospec]
    return pl.pallas_call(
        functools.partial(_rwkv_prep_kernel, tiles_per_seq=seq // tm),
        grid=(t // tm,),
        in_specs=[xspec(0), xspec(1), xspec(2), pl.BlockSpec((tm, xw2), lambda i: (i, 0)),
                  hspec(0), hspec(1), hspec(2),
                  pl.BlockSpec((SUBLANES, xw2), lambda i: (jnp.maximum(i * hb - 1, 0), 0)),
                  row(w, 0), row(w, 1), row(w, 2),
                  pl.BlockSpec((None, 1, xw2), lambda i: (layer, 0, 3 * w // xw2)),
                  row(w, 0),
                  pl.BlockSpec((None, LANES, w), lambda i: (layer, 0, 0)),
                  row(w, 0),
                  pl.BlockSpec((None, LANES, w), lambda i: (layer, 0, 0)),
                  pl.BlockSpec((None, RWKV_G_RANK, w), lambda i: (layer, 0, 0)),
                  row(w, 0), row(w, 0), row(w, 0)],
        out_specs=out_specs,
        out_shape=outs,
        compiler_params=_cparams("arbitrary"),
        name="rwkv_prep",
    )(p1, p1, p1, p2, p1, p1, p1, p2, mu3, mu3, mu3, mu3, w0[:, None, :], wup_p, a0[:, None, :], aup_p,
      gup_b, k_k[:, None, :], k_a[:, None, :], rk_row)


def _rwkv_chunk_kernel(kkd_ref, rt_ref, kt_ref, bt_ref, kh_ref, bh_ref, v_ref, pc_ref, y_ref, st_ref,
                       n_scr, arb_scr, uloc_scr, yloc_scr):
    nb, tb, w = kkd_ref.shape
    c = CHUNK
    nck = tb // c
    n2 = 2 * tb
    npair = w // LANES

    @pl.when(pl.program_id(0) == 0)
    def _():
        st_ref[...] = jnp.zeros(st_ref.shape, F32)

    r = _iota((n2, n2), 0)
    cc = _iota((n2, n2), 1)
    same = (r // c) == (cc // c)
    incl = same & (r >= cc)
    strict = same & (r > cc)
    head0 = _iota((1, LANES), 1) < RWKV_HEAD
    bd = (_iota((LANES, LANES), 0) // RWKV_HEAD) == (_iota((LANES, LANES), 1) // RWKV_HEAD)
    units = [(b, p) for b in range(nb) for p in range(npair)]

    def cols(p):
        return slice(p * LANES, (p + 1) * LANES)

    def rows(ci):
        return slice(ci * c, (ci + 1) * c)

    def blk(ci):
        return slice(ci * LANES, (ci + 1) * LANES)

    def stack_heads(x):
        x0 = jnp.where(head0, x, jnp.zeros_like(x))
        x1 = jnp.where(head0, jnp.zeros_like(x), x)
        return jnp.concatenate([piece[rows(ci)] for ci in range(nck) for piece in (x0, x1)], axis=0)

    def stack_dup(x):
        return jnp.concatenate([x[rows(ci)] for ci in range(nck) for _ in range(2)], axis=0)

    for g0 in range(0, len(units), GROUP):
        grp = units[g0:g0 + GROUP]
        lks = [stack_heads(kkd_ref[b, :, cols(p)]) for b, p in grp]
        lrs = [stack_heads(rt_ref[b, :, cols(p)]) for b, p in grp]
        rks = [stack_dup(kt_ref[b, :, cols(p)]) for b, p in grp]
        rbs = [stack_dup(bt_ref[b, :, cols(p)]) for b, p in grp]
        v2s = [stack_dup(v_ref[b, :, cols(p)]) for b, p in grp]
        a_kb = [jnp.where(strict, _dot_nt(lk, rb), 0.0) for lk, rb in zip(lks, rbs)]
        a_kk = [jnp.where(strict, _dot_nt(lk, rk), 0.0) for lk, rk in zip(lks, rks)]
        ns = _tri_inv_m1(a_kb, c)
        avs = [_dot(a, v2) for a, v2 in zip(a_kk, v2s)]
        for idx in range(len(grp)):
            u = g0 + idx
            nbf = ns[idx].astype(BF16)
            n_scr[u] = nbf
            uloc_scr[u] = avs[idx] + _dot(nbf, avs[idx])
            a_rk = jnp.where(incl, _dot_nt(lrs[idx], rks[idx]), 0.0)
            yloc_scr[u] = _dot(a_rk, v2s[idx])
            arb_scr[u] = jnp.where(incl, _dot_nt(lrs[idx], rbs[idx]), 0.0).astype(BF16)

    nu = len(units)
    states = [st_ref[u] for u in range(nu)]
    rs_parts = [[] for _ in units]
    ub_parts = [[] for _ in units]
    for ci in range(nck):
        sbs = [s.astype(BF16) for s in states]
        kss = [_dot_nt(jnp.concatenate([kkd_ref[b, rows(ci), cols(p)], rt_ref[b, rows(ci), cols(p)]], axis=0), sbs[u])
               for u, (b, p) in enumerate(units)]
        ks2 = [jnp.concatenate([ks[:c], ks[:c]], axis=0) for ks in kss]
        tks = [ks2[u] + _dot(n_scr[u, blk(ci), blk(ci)], ks2[u]) for u in range(nu)]
        for u in range(nu):
            u2 = uloc_scr[u, blk(ci), :] + tks[u]
            ub_parts[u].append(jnp.where(head0, u2[:c], u2[c:]).astype(BF16))
            rs_parts[u].append(kss[u][c:])
        states = [states[u] * pc_ref[b, ci, :, cols(p)]
                  + jnp.where(bd, _dot_tn(jnp.concatenate([v_ref[b, rows(ci), cols(p)], ub_parts[u][ci]], axis=0),
                                          jnp.concatenate([kh_ref[b, rows(ci), cols(p)],
                                                           -bh_ref[b, rows(ci), cols(p)]], axis=0)), 0.0)
                  for u, (b, p) in enumerate(units)]

    for u, (b, p) in enumerate(units):
        u2all = jnp.concatenate([ub_parts[u][ci] for ci in range(nck) for _ in range(2)], axis=0)
        yall = yloc_scr[u] - _dot(arb_scr[u], u2all)
        for ci in range(nck):
            yb = yall[blk(ci), :]
            y_ref[b, rows(ci), cols(p)] = rs_parts[u][ci] + jnp.where(head0, yb[:c], yb[c:])
        st_ref[u] = states[u]


def rwkv_chunk(tensors, pc, nb, seq):
    t, w = tensors[0].shape
    tb = min(RWKV_STEP, seq)
    nck = tb // CHUNK
    nu = nb * (w // LANES)
    xspec = pl.BlockSpec((nb, tb, w), lambda s: (0, s, 0))
    out = pl.pallas_call(
        _rwkv_chunk_kernel,
        grid=(seq // tb,),
        in_specs=[xspec] * 7 + [pl.BlockSpec((nb, nck, 1, w), lambda s: (0, s, 0, 0))],
        out_specs=xspec,
        out_shape=jax.ShapeDtypeStruct((nb, seq, w), F32),
        scratch_shapes=[pltpu.VMEM((nu, LANES, LANES), F32),
                        pltpu.VMEM((nu, 2 * tb, 2 * tb), BF16),
                        pltpu.VMEM((nu, 2 * tb, 2 * tb), BF16),
                        pltpu.VMEM((nu, 2 * tb, LANES), F32),
                        pltpu.VMEM((nu, 2 * tb, LANES), F32)],
        compiler_params=_cparams("arbitrary"),
        name="rwkv_chunk",
    )(*[x.reshape(nb, seq, w) for x in tensors], pc.reshape(nb, seq // CHUNK, 1, w))
    return out.reshape(t, w)


def _rwkv_post_kernel(y_ref, bonus_ref, g_ref, lw_ref, lb_ref, o_ref):
    hm = _head_sum_matrix() * (1.0 / RWKV_HEAD)
    hm = hm.astype(BF16)
    for blk in range(RWKV_WIDTH // LANES):
        cols = slice(blk * LANES, (blk + 1) * LANES)
        y = y_ref[:, cols]
        mean = _dot_exact_rhs(y, hm, 3)
        yc = y - mean
        var = _dot_exact_rhs(yc * yc, hm, 3)
        yn = yc * lax.rsqrt(var + RWKV_LN_EPS) * lw_ref[:, cols] + lb_ref[:, cols]
        out = (yn + bonus_ref[:, cols].astype(F32)) * g_ref[:, cols].astype(F32)
        o_ref[:, cols] = out.astype(o_ref.dtype)


def rwkv_post(y, bonus, g, lnx_w, lnx_b, layer, tm):
    t, w = y.shape
    tm = min(tm, t)
    xspec = pl.BlockSpec((tm, w), lambda i: (i, 0))
    rspec = pl.BlockSpec((None, 1, w), lambda i: (layer, 0, 0))
    return pl.pallas_call(
        _rwkv_post_kernel,
        grid=(t // tm,),
        in_specs=[xspec, xspec, xspec, rspec, rspec],
        out_specs=xspec,
        out_shape=jax.ShapeDtypeStruct((t, w), BF16),
        compiler_params=_cparams("arbitrary"),
        name="rwkv_post",
    )(y, bonus, g, lnx_w[:, None, :], lnx_b[:, None, :])


def _relayout_w_in(w_in):
    nl, d, _ = w_in.shape
    wt = jnp.swapaxes(w_in, 1, 2).astype(BF16)

    def rows(lo, n):
        return wt[:, lo:lo + n, :]

    def zeros(n):
        return jnp.zeros((nl, n, d), BF16)

    kr = rows(_O_KR, MLA_ROPE)
    half = MLA_ROPE // 2
    kr_swap = jnp.concatenate([-kr[:, half:, :], kr[:, :half, :]], axis=1)
    w1 = jnp.concatenate([
        rows(_O_CQ, MLA_Q_RANK),
        kr, zeros(LANES - MLA_ROPE), kr_swap, zeros(LANES - MLA_ROPE),
        rows(_O_CKV, MLA_KV_RANK), zeros(GQ_OFF - CKV_OFF - MLA_KV_RANK),
        rows(_O_GDN, 4096),
        rows(_O_RWKV, 3072),
        rows(_O_GATE, 3 * D_MODEL)], axis=1)
    w2 = jnp.concatenate([rows(_O_XW, 256), rows(_O_GB, 16), zeros(NP2 - 272)], axis=1)
    return w1, w2


def _relayout_w_uq(w_uq):
    nl, r, _ = w_uq.shape
    wq = w_uq.reshape(nl, r, MLA_HEADS, MLA_NOPE + MLA_ROPE)
    nope = wq[..., :MLA_NOPE]
    rope = wq[..., MLA_NOPE:]
    half = MLA_ROPE // 2
    swap = jnp.concatenate([-rope[..., half:], rope[..., :half]], axis=-1)
    z = jnp.zeros(rope.shape[:-1] + (LANES - MLA_ROPE,), w_uq.dtype)
    return jnp.concatenate([nope, rope, z, swap, z], axis=-1).reshape(nl, r, 3 * MLA_HEADS * LANES).astype(BF16)


def _relayout_w_ukv(w_ukv):
    nl, r, _ = w_ukv.shape
    wkv = w_ukv.reshape(nl, r, MLA_HEADS, MLA_NOPE + MLA_V)
    wk = wkv[..., :MLA_NOPE].reshape(nl, r, MLA_HEADS * MLA_NOPE)
    wv = wkv[..., MLA_NOPE:].reshape(nl, r, MLA_HEADS * MLA_V)
    return jnp.concatenate([wk, wv], axis=-1).astype(BF16)


def _rope_table(positions):
    inv = 1.0 / (ROPE_THETA ** (jnp.arange(0, MLA_ROPE, 2, dtype=F32) / MLA_ROPE))
    ang = positions.astype(F32)[..., None] * inv
    cos, sin = jnp.cos(ang), jnp.sin(ang)
    z = jnp.zeros(cos.shape[:-1] + (LANES - MLA_ROPE,), F32)
    cs = jnp.concatenate([cos, cos, z, sin, sin, z], axis=-1)
    return cs.reshape(-1, 2 * LANES)


def kernel(x, c, positions, w_ada, b_ada, norm1_w, w_in, mla_q_norm_w, mla_w_uq, mla_kv_norm_w, mla_w_ukv,
           gdn_conv_w, gdn_a_log, gdn_dt_bias, gdn_norm_w, rwkv_mu, rwkv_w0, rwkv_w_up, rwkv_a0, rwkv_a_up,
           rwkv_g_up, rwkv_k_k, rwkv_k_a, rwkv_r_k, rwkv_lnx_w, rwkv_lnx_b, w_branch, w_out, norm2_w,
           w_gate_up, w_down, final_norm_w):
    nb, seq, d = x.shape
    nl = w_in.shape[0]
    assert seq % (2 * CHUNK) == 0 and d == D_MODEL

    w1, w2 = _relayout_w_in(w_in)
    wq3 = _relayout_w_uq(mla_w_uq)
    wkv = _relayout_w_ukv(mla_w_ukv)
    w_branch_b = w_branch.astype(BF16)
    w_out_b = w_out.astype(BF16)
    w_gu_b = w_gate_up.astype(BF16)
    w_down_b = w_down.astype(BF16)
    cs = _rope_table(positions)

    mod = adaln_mod(c, w_ada, b_ada)
    modr = mod.reshape(nl * nb * 6, 1, d)
    n1 = norm1_w[:, None, :]
    n2 = norm2_w[:, None, :]
    qnw = mla_q_norm_w[:, None, :]
    kvnw = mla_kv_norm_w[:, None, :]
    gnw = gdn_norm_w[:, None, :]

    xf = x.reshape(nb * seq, d)
    for l in range(nl):
        p1 = norm_gemm(xf, n1, modr, w1, l, 0, nb, seq, BF16, *GEMM_TILES["w_in"])
        p2 = norm_gemm(xf, n1, modr, w2, l, 0, nb, seq, F32, *GEMM_TILES["w_in_f32"])
        q, k, vt = mla_proj(p1, cs, qnw, kvnw, wq3, wkv, l, nb, seq, ROW_TILE)
        o_a = flash_attn(q, k, vt, nb, seq, ATTN_TILE)
        qn, kn, vc, bg = gdn_prep(p1, p2, gdn_conv_w, gdn_a_log, gdn_dt_bias, l, seq, ROW_TILE)
        o_b = gdn_chunk(qn, kn, vc, p1, bg, gnw, l, nb, seq)
        prep = rwkv_prep(p1, p2, rwkv_mu, rwkv_w0, rwkv_w_up, rwkv_a0, rwkv_a_up, rwkv_g_up,
                         rwkv_k_k, rwkv_k_a, rwkv_r_k, l, seq, ROW_TILE)
        y = rwkv_chunk(prep[:7], prep[7], nb, seq)
        o_c = rwkv_post(y, prep[8], prep[9], rwkv_lnx_w, rwkv_lnx_b, l, ROW_TILE)
        merged = merge_gemm(o_a, o_b, o_c, w_branch_b, p1, l, *GEMM_TILES["merge"])
        xf = resid_gemm(merged, w_out_b, xf, modr, l, 2, nb, seq, *GEMM_TILES["w_out"])
        act = ffn_up(xf, n2, modr, w_gu_b, l, nb, seq, *GEMM_TILES["ffn_up"])
        xf = resid_gemm(act, w_down_b, xf, modr, l, 5, nb, seq, *GEMM_TILES["ffn_down"])
    return final_norm(xf, final_norm_w, ROW_TILE).reshape(nb, seq, d)
```

```python
import functools
import math

import jax
import jax.numpy as jnp
from jax import lax
from jax.experimental import pallas as pl
from jax.experimental.pallas import tpu as pltpu

F32 = jnp.float32
BF16 = jnp.bfloat16

D_MODEL = 2048
MLA_HEADS = 8
MLA_Q_RANK = 768
MLA_KV_RANK = 512
MLA_NOPE = 128
MLA_ROPE = 64
MLA_V = 128
ROPE_THETA = 10000.0
GDN_HEADS = 8
GDN_DK = 128
GDN_DV = 128
GDN_CONV = 4
RWKV_HEADS = 16
RWKV_HEAD = 64
RWKV_WIDTH = RWKV_HEADS * RWKV_HEAD
RWKV_W_RANK = 64
RWKV_A_RANK = 64
RWKV_G_RANK = 128
RWKV_LN_EPS = 64e-5
D_FF = 5632
NORM_EPS = 1e-6
CHUNK = 64
MXU_TILE = 256
GDN_STEP = MXU_TILE
RWKV_STEP = MXU_TILE // 2
GROUP = 4
FLASH_HEADS = 8
VT_PAD = 16
LANES = 128
SUBLANES = 8
VMEM_LIMIT = 56 * 1024 * 1024

CQ_OFF = 0
KR_OFF = 768
CKV_OFF = 1024
GQ_OFF = 2048
RR_OFF = 6144
GATE_OFF = 9216
NP1 = 15360
NP2 = 384

GEMM_TILES = {
    "w_in": (1024, 1536),
    "merge": (1024, 1024),
    "w_out": (1024, 1024),
    "ffn_up": (1024, 512),
    "ffn_down": (1024, 512),
}
ROW_TILE = 512
ATTN_TILE = 512

_O_CQ = 0
_O_CKV = 768
_O_KR = 1280
_O_GDN = 1344
_O_GZ = _O_GDN + 3072
_O_GB = _O_GZ + 1024
_O_GA = _O_GB + 8
_O_RWKV = _O_GA + 8
_O_XW = _O_RWKV + 3072
_O_XA = _O_XW + 64
_O_XG = _O_XA + 64
_O_GATE = _O_XG + 128


def _cparams(*sem):
    return pltpu.CompilerParams(dimension_semantics=sem, vmem_limit_bytes=VMEM_LIMIT)


def _dot(a, b):
    return jnp.dot(a.astype(BF16), b.astype(BF16), preferred_element_type=F32)


def _dot_nt(a, b):
    return lax.dot_general(a.astype(BF16), b.astype(BF16), (((1,), (1,)), ((), ())),
                           preferred_element_type=F32)


def _dot_tn(a, b):
    return lax.dot_general(a.astype(BF16), b.astype(BF16), (((0,), (0,)), ((), ())),
                           preferred_element_type=F32)


def _split(a, terms):
    parts = []
    rem = a
    for _ in range(terms):
        p = rem.astype(BF16)
        parts.append(p)
        rem = rem - p.astype(F32)
    return parts


def _dot_exact_lhs(a_bf, b, terms):
    out = None
    for p in _split(b, terms):
        t = _dot(a_bf, p)
        out = t if out is None else out + t
    return out


def _dot_exact_rhs(a, b_bf, terms):
    out = None
    for p in _split(a, terms):
        t = _dot(p, b_bf)
        out = t if out is None else out + t
    return out


def _iota(shape, dim):
    return lax.broadcasted_iota(jnp.int32, shape, dim)


def _tri_inv_m1(a_list, n):
    m = a_list[0].shape[0]
    nblk = m // n
    lane_blk = _iota((n, m), 1) // n

    def compact(x):
        out = x[0:n]
        for i in range(1, nblk):
            out = out + x[i * n:(i + 1) * n]
        return out

    def expand(xc):
        zero = jnp.zeros_like(xc)
        return jnp.concatenate([jnp.where(lane_blk == i, xc, zero) for i in range(nblk)], axis=0)

    a_bf = [a.astype(BF16) for a in a_list]
    acs = [compact(a) for a in a_list]
    ns = [-ac for ac in acs]
    ps = [_dot(ac, ab) for ac, ab in zip(acs, a_bf)]
    k = 2
    while True:
        pbds = [expand(p.astype(BF16)) for p in ps]
        both = [_dot(jnp.concatenate([nn, p], axis=0), pbd) for nn, p, pbd in zip(ns, ps, pbds)]
        ns = [nn + p + bo[0:n] for nn, p, bo in zip(ns, ps, both)]
        k *= 2
        if k >= n:
            break
        ps = [bo[n:2 * n] for bo in both]
    return [expand(nn) for nn in ns]


def _sigmoid(x):
    return jax.nn.sigmoid(x)


def _softplus(x):
    return jnp.maximum(x, 0.0) + jnp.log1p(jnp.exp(-jnp.abs(x)))


def _mod_kernel(c_ref, w_ref, b_ref, o_ref):
    w = w_ref[...]
    for m in range(c_ref.shape[0]):
        c = c_ref[m]
        ca = c * _sigmoid(c)
        o_ref[m:m + 1, :] = jnp.sum(ca * w, axis=0, keepdims=True) + b_ref[...]


def adaln_mod(c, w_ada, b_ada):
    nl, d, n = w_ada.shape
    b = c.shape[0]
    tn = min(1024, n)
    return pl.pallas_call(
        _mod_kernel,
        grid=(nl, n // tn),
        in_specs=[pl.BlockSpec((b, d, 1), lambda l, j: (0, 0, 0)),
                  pl.BlockSpec((None, d, tn), lambda l, j: (l, 0, j)),
                  pl.BlockSpec((None, 1, tn), lambda l, j: (l, 0, j))],
        out_specs=pl.BlockSpec((None, b, tn), lambda l, j: (l, 0, j)),
        out_shape=jax.ShapeDtypeStruct((nl, b, n), F32),
        compiler_params=_cparams("arbitrary", "arbitrary"),
        name="adaln_mod",
    )(c[:, :, None], w_ada, b_ada[:, None, :])


def _modulate_to(h_scr, x_ref, nw_ref, sh_ref, sc_ref):
    x = x_ref[...]
    ms = jnp.mean(x * x, axis=-1, keepdims=True)
    y = x * lax.rsqrt(ms + NORM_EPS) * nw_ref[...]
    h_scr[...] = (y * (1.0 + sc_ref[...]) + sh_ref[...]).astype(BF16)


def _norm_gemm_kernel(x_ref, nw_ref, sh_ref, sc_ref, w_ref, w2_ref, o_ref, o2_ref, h_scr):
    @pl.when(pl.program_id(1) == 0)
    def _():
        _modulate_to(h_scr, x_ref, nw_ref, sh_ref, sc_ref)
        o2_ref[...] = _dot_nt(h_scr[...], w2_ref[...])

    o_ref[...] = _dot_nt(h_scr[...], w_ref[...]).astype(o_ref.dtype)


def _mod_spec(d, layer, nb, which, tiles_per_seq):
    return pl.BlockSpec((None, 1, d),
                        lambda i, j: ((layer * nb + i // tiles_per_seq) * 6 + which, 0, 0))


def norm_gemm(x, norm_w, modr, w, w2, layer, which_sh, nb, seq, tm, tn):
    t, d = x.shape
    n = w.shape[1]
    n2 = w2.shape[1]
    tm = min(tm, seq)
    tn = min(tn, n)
    tps = seq // tm
    return pl.pallas_call(
        _norm_gemm_kernel,
        grid=(t // tm, n // tn),
        in_specs=[pl.BlockSpec((tm, d), lambda i, j: (i, 0)),
                  pl.BlockSpec((None, 1, d), lambda i, j: (layer, 0, 0)),
                  _mod_spec(d, layer, nb, which_sh, tps),
                  _mod_spec(d, layer, nb, which_sh + 1, tps),
                  pl.BlockSpec((None, tn, d), lambda i, j: (layer, j, 0)),
                  pl.BlockSpec((None, n2, d), lambda i, j: (layer, 0, 0))],
        out_specs=[pl.BlockSpec((tm, tn), lambda i, j: (i, j)),
                   pl.BlockSpec((tm, n2), lambda i, j: (i, 0))],
        out_shape=[jax.ShapeDtypeStruct((t, n), BF16), jax.ShapeDtypeStruct((t, n2), F32)],
        scratch_shapes=[pltpu.VMEM((tm, d), BF16)],
        compiler_params=_cparams("arbitrary", "arbitrary"),
        name="norm_gemm",
    )(x, norm_w, modr, modr, w, w2)


def _ffn_up_kernel(x_ref, nw_ref, sh_ref, sc_ref, wg_ref, wu_ref, o_ref, h_scr):
    @pl.when(pl.program_id(1) == 0)
    def _():
        _modulate_to(h_scr, x_ref, nw_ref, sh_ref, sc_ref)

    h = h_scr[...]
    gate = jnp.dot(h, wg_ref[...], preferred_element_type=F32)
    up = jnp.dot(h, wu_ref[...], preferred_element_type=F32)
    o_ref[...] = (gate * _sigmoid(gate) * up).astype(o_ref.dtype)


def ffn_up(x, norm_w, modr, w_gu, layer, nb, seq, tm, tn):
    t, d = x.shape
    f = w_gu.shape[-1] // 2
    tm = min(tm, seq)
    tn = min(tn, f)
    tps = seq // tm
    nf = f // tn
    return pl.pallas_call(
        _ffn_up_kernel,
        grid=(t // tm, nf),
        in_specs=[pl.BlockSpec((tm, d), lambda i, j: (i, 0)),
                  pl.BlockSpec((None, 1, d), lambda i, j: (layer, 0, 0)),
                  _mod_spec(d, layer, nb, 3, tps),
                  _mod_spec(d, layer, nb, 4, tps),
                  pl.BlockSpec((None, d, tn), lambda i, j: (layer, 0, j)),
                  pl.BlockSpec((None, d, tn), lambda i, j: (layer, 0, j + nf))],
        out_specs=pl.BlockSpec((tm, tn), lambda i, j: (i, j)),
        out_shape=jax.ShapeDtypeStruct((t, f), BF16),
        scratch_shapes=[pltpu.VMEM((tm, d), BF16)],
        compiler_params=_cparams("arbitrary", "arbitrary"),
        name="ffn_up",
    )(x, norm_w, modr, modr, w_gu, w_gu)


def _resid_gemm_kernel(a_ref, w_ref, x_ref, gt_ref, o_ref):
    y = jnp.dot(a_ref[...], w_ref[...], preferred_element_type=F32)
    o_ref[...] = x_ref[...] + gt_ref[...] * y


def resid_gemm(a, w, x, modr, layer, which_gt, nb, seq, tm, tn):
    t, k = a.shape
    d = x.shape[-1]
    tm = min(tm, seq)
    tn = min(tn, d)
    tps = seq // tm
    return pl.pallas_call(
        _resid_gemm_kernel,
        grid=(t // tm, d // tn),
        in_specs=[pl.BlockSpec((tm, k), lambda i, j: (i, 0)),
                  pl.BlockSpec((None, k, tn), lambda i, j: (layer, 0, j)),
                  pl.BlockSpec((tm, tn), lambda i, j: (i, j)),
                  pl.BlockSpec((None, 1, tn),
                               lambda i, j: ((layer * nb + i // tps) * 6 + which_gt, 0, j))],
        out_specs=pl.BlockSpec((tm, tn), lambda i, j: (i, j)),
        out_shape=jax.ShapeDtypeStruct((t, d), F32),
        compiler_params=_cparams("arbitrary", "arbitrary"),
        name="resid_gemm",
    )(a, w, x, modr)


def _merge_kernel(oa_ref, ob_ref, oc_ref, wa_ref, wb_ref, wc_ref, ga_ref, gb_ref, gc_ref, o_ref):
    def branch(o_r, w_r, g_r):
        y = jnp.dot(o_r[...], w_r[...], preferred_element_type=F32)
        return _sigmoid(g_r[...].astype(F32)) * y

    acc = branch(oa_ref, wa_ref, ga_ref) + branch(ob_ref, wb_ref, gb_ref) + branch(oc_ref, wc_ref, gc_ref)
    o_ref[...] = acc.astype(o_ref.dtype)


def merge_gemm(o_a, o_b, o_c, w_branch, p1, layer, tm, tn):
    t, kb = o_a.shape
    d = w_branch.shape[-1]
    tm = min(tm, t)
    tn = min(tn, d)
    gblk = GATE_OFF // tn
    dblk = d // tn

    def ospec():
        return pl.BlockSpec((tm, kb), lambda i, j: (i, 0))

    def wspec(r):
        return pl.BlockSpec((None, kb, tn), lambda i, j: (layer, r, j))

    def gspec(r):
        return pl.BlockSpec((tm, tn), lambda i, j: (i, gblk + r * dblk + j))

    return pl.pallas_call(
        _merge_kernel,
        grid=(t // tm, d // tn),
        in_specs=[ospec(), ospec(), ospec(), wspec(0), wspec(1), wspec(2), gspec(0), gspec(1), gspec(2)],
        out_specs=pl.BlockSpec((tm, tn), lambda i, j: (i, j)),
        out_shape=jax.ShapeDtypeStruct((t, d), BF16),
        compiler_params=_cparams("arbitrary", "arbitrary"),
        name="merge_gemm",
    )(o_a, o_b, o_c, w_branch, w_branch, w_branch, p1, p1, p1)


def _final_norm_kernel(x_ref, w_ref, o_ref):
    x = x_ref[...]
    ms = jnp.mean(x * x, axis=-1, keepdims=True)
    o_ref[...] = x * lax.rsqrt(ms + NORM_EPS) * w_ref[...]


def final_norm(x, w, tm):
    t, d = x.shape
    tm = min(tm, t)
    return pl.pallas_call(
        _final_norm_kernel,
        grid=(t // tm,),
        in_specs=[pl.BlockSpec((tm, d), lambda i: (i, 0)), pl.BlockSpec((1, d), lambda i: (0, 0))],
        out_specs=pl.BlockSpec((tm, d), lambda i: (i, 0)),
        out_shape=jax.ShapeDtypeStruct((t, d), F32),
        compiler_params=_cparams("arbitrary"),
        name="final_norm",
    )(x, w[None, :])


def _rms_rows(x, w):
    ms = jnp.mean(x * x, axis=-1, keepdims=True)
    return x * lax.rsqrt(ms + NORM_EPS) * w


def _mla_proj_kernel(cq_ref, kr_ref, ckv_ref, cs_ref, qnw_ref, kvnw_ref, wq_ref, wkv_ref,
                     q_ref, k_ref, vt_ref):
    scale =(MLA_NOPE + MLA_ROPE) ** -0.5 * math.log2(math.e)
    cc = cs_ref[:, 0:LANES]
    ss = cs_ref[:, LANES:2 * LANES]
    nq = _rms_rows(cq_ref[...].astype(F32), qnw_ref[...]).astype(BF16)
    q3 = jnp.dot(nq, wq_ref[...], preferred_element_type=F32)
    for h in range(MLA_HEADS):
        b0 = h * 3 * LANES
        nope = q3[:, b0:b0 + LANES]
        rope = q3[:, b0 + LANES:b0 + 2 * LANES] * cc + q3[:, b0 + 2 * LANES:b0 + 3 * LANES] * ss
        q_ref[:, 2 * h * LANES:(2 * h + 1) * LANES] = (nope * scale).astype(BF16)
        q_ref[:, (2 * h + 1) * LANES:(2 * h + 2) * LANES] = (rope * scale).astype(BF16)
    nkv = _rms_rows(ckv_ref[...].astype(F32), kvnw_ref[...]).astype(BF16)
    kv = jnp.dot(nkv, wkv_ref[...], preferred_element_type=F32)
    kr = kr_ref[...].astype(F32)
    krope = (kr[:, 0:LANES] * cc + kr[:, LANES:2 * LANES] * ss).astype(BF16)
    for h in range(MLA_HEADS):
        k_ref[:, 2 * h * LANES:(2 * h + 1) * LANES] = kv[:, h * LANES:(h + 1) * LANES].astype(BF16)
        k_ref[:, (2 * h + 1) * LANES:(2 * h + 2) * LANES] = krope
    ones = jnp.ones((VT_PAD, kv.shape[0]), BF16)
    for h in range(MLA_HEADS):
        vh = kv[:, (MLA_HEADS + h) * LANES:(MLA_HEADS + h + 1) * LANES]
        vt_ref[h, 0:MLA_V, :] = vh.T.astype(BF16)
        vt_ref[h, MLA_V:MLA_V + VT_PAD, :] = ones


def mla_proj(p1, cs, q_norm_w, kv_norm_w, wq3, wkv, layer, nb, seq, tm):
    t = p1.shape[0]
    tm = min(tm, seq)
    tps = seq // tm
    hq = MLA_HEADS * 2 * LANES
    return pl.pallas_call(
        _mla_proj_kernel,
        grid=(t // tm,),
        in_specs=[pl.BlockSpec((tm, MLA_Q_RANK), lambda i: (i, CQ_OFF // MLA_Q_RANK)),
                  pl.BlockSpec((tm, 2 * LANES), lambda i: (i, KR_OFF // (2 * LANES))),
                  pl.BlockSpec((tm, MLA_KV_RANK), lambda i: (i, CKV_OFF // MLA_KV_RANK)),
                  pl.BlockSpec((tm, 2 * LANES), lambda i: (i, 0)),
                  pl.BlockSpec((None, 1, MLA_Q_RANK), lambda i: (layer, 0, 0)),
                  pl.BlockSpec((None, 1, MLA_KV_RANK), lambda i: (layer, 0, 0)),
                  pl.BlockSpec((None, MLA_Q_RANK, 3 * MLA_HEADS * LANES), lambda i: (layer, 0, 0)),
                  pl.BlockSpec((None, MLA_KV_RANK, 2 * MLA_HEADS * LANES), lambda i: (layer, 0, 0))],
        out_specs=[pl.BlockSpec((tm, hq), lambda i: (i, 0)),
                   pl.BlockSpec((tm, hq), lambda i: (i, 0)),
                   pl.BlockSpec((None, MLA_HEADS, MLA_V + VT_PAD, tm), lambda i: (i // tps, 0, 0, i % tps))],
        out_shape=[jax.ShapeDtypeStruct((t, hq), BF16),
                   jax.ShapeDtypeStruct((t, hq), BF16),
                   jax.ShapeDtypeStruct((nb, MLA_HEADS, MLA_V + VT_PAD, seq), BF16)],
        compiler_params=_cparams("arbitrary"),
        name="mla_proj",
    )(p1, p1, p1, cs, q_norm_w, kv_norm_w, wq3, wkv)


def _flash_kernel(it_ref, jt_ref, q_ref, k_ref, vt_ref, o_ref, m_scr, acc_scr):
    t = pl.program_id(2)
    i = it_ref[t]
    j = jt_ref[t]
    heads = range(FLASH_HEADS)

    @pl.when(j == 0)
    def _():
        m_scr[...] = jnp.full(m_scr.shape, -jnp.inf, F32)
        acc_scr[...] = jnp.zeros(acc_scr.shape, F32)

    def step(masked):
        sts = [lax.dot_general(k_ref[:, 2 * g * LANES:2 * (g + 1) * LANES],
                               q_ref[:, 2 * g * LANES:2 * (g + 1) * LANES],
                               (((1,), (1,)), ((), ())), preferred_element_type=F32) for g in heads]
        if masked:
            keep = _iota(sts[0].shape, 1) >= _iota(sts[0].shape, 0)
            sts = [jnp.where(keep, s, -jnp.inf) for s in sts]
        m_prev = [m_scr[g] for g in heads]
        m_new = [jnp.maximum(mp, jnp.max(s, axis=0, keepdims=True)) for mp, s in zip(m_prev, sts)]
        pts = [jnp.exp2(s - mn).astype(BF16) for s, mn in zip(sts, m_new)]
        alphas = [jnp.exp2(mp - mn) for mp, mn in zip(m_prev, m_new)]
        pvs = [jnp.dot(vt_ref[g], pt, preferred_element_type=F32) for g, pt in zip(heads, pts)]
        for g in heads:
            acc_scr[g] = alphas[g] * acc_scr[g] + pvs[g]
            m_scr[g] = m_new[g]

    @pl.when(j < i)
    def _():
        step(False)

    @pl.when(j == i)
    def _():
        step(True)
        for g in heads:
            acc = acc_scr[g]
            o_t = acc[0:MLA_V, :] / acc[MLA_V:MLA_V + 1, :]
            o_ref[:, g * LANES:(g + 1) * LANES] = o_t.T.astype(o_ref.dtype)


def flash_attn(q, k, vt, nb, seq, tq):
    tq = min(tq, seq)
    nq = seq // tq
    hg = FLASH_HEADS
    pairs = [(i, j) for i in range(nq) for j in range(i + 1)]
    it = jnp.asarray([p[0] for p in pairs], jnp.int32)
    jt = jnp.asarray([p[1] for p in pairs], jnp.int32)
    grid_spec = pltpu.PrefetchScalarGridSpec(
        num_scalar_prefetch=2,
        grid=(nb, MLA_HEADS // hg, len(pairs)),
        in_specs=[pl.BlockSpec((tq, hg * 2 * LANES), lambda b, h, t, it_r, jt_r: (b * nq + it_r[t], h)),
                  pl.BlockSpec((tq, hg * 2 * LANES), lambda b, h, t, it_r, jt_r: (b * nq + jt_r[t], h)),
                  pl.BlockSpec((None, hg, MLA_V + VT_PAD, tq), lambda b, h, t, it_r, jt_r: (b, h, 0, jt_r[t]))],
        out_specs=pl.BlockSpec((tq, hg * LANES), lambda b, h, t, it_r, jt_r: (b * nq + it_r[t], h)),
        scratch_shapes=[pltpu.VMEM((hg, 1, tq), F32), pltpu.VMEM((hg, MLA_V + VT_PAD, tq), F32)],
    )
    return pl.pallas_call(
        _flash_kernel,
        grid_spec=grid_spec,
        out_shape=jax.ShapeDtypeStruct((nb * seq, MLA_HEADS * LANES), BF16),
        compiler_params=_cparams("arbitrary", "arbitrary", "arbitrary"),
        name="flash_attn",
    )(it, jt, q, k, vt)


def _shifted_rows(x, halo, d):
    xr = pltpu.roll(x, d, 0)
    hr = pltpu.roll(halo, d, 0)
    top = jnp.where(_iota(halo.shape, 0) < d, hr, xr[0:SUBLANES])
    return jnp.concatenate([top, xr[SUBLANES:]], axis=0)


def _gdn_prep_kernel(q_ref, k_ref, v_ref, qh_ref, kh_ref, vh_ref, cwq_ref, cwk_ref, cwv_ref,
                     p2_ref, alog_ref, dtb_ref, qo_ref, ko_ref, vo_ref, bg_ref, *, tiles_per_seq):
    first = (pl.program_id(0) % tiles_per_seq) == 0

    def conv_silu(x_ref, h_ref, w_ref):
        x = x_ref[...].astype(F32)
        halo = jnp.where(first, 0.0, h_ref[...].astype(F32))
        w = w_ref[...]
        acc = x * w[GDN_CONV - 1:GDN_CONV, :]
        for d in range(1, GDN_CONV):
            acc = acc + _shifted_rows(x, halo, d) * w[GDN_CONV - 1 - d:GDN_CONV - d, :]
        return acc * _sigmoid(acc)

    def l2n(x, mult):
        outs = []
        for h in range(GDN_HEADS):
            xh = x[:, h * LANES:(h + 1) * LANES]
            ss = jnp.sum(xh * xh, axis=-1, keepdims=True)
            outs.append(xh * (lax.rsqrt(ss + 1e-6) * mult))
        return jnp.concatenate(outs, axis=1)

    qo_ref[...] = l2n(conv_silu(q_ref, qh_ref, cwq_ref), GDN_DK ** -0.5).astype(BF16)
    ko_ref[...] = l2n(conv_silu(k_ref, kh_ref, cwk_ref), 1.0).astype(BF16)
    vo_ref[...] = conv_silu(v_ref, vh_ref, cwv_ref).astype(BF16)
    p2 = p2_ref[...]
    lane = _iota(p2.shape, 1)
    g = -jnp.exp(alog_ref[...]) * _softplus(p2 + dtb_ref[...])
    bg_ref[...] = jnp.where(lane < GDN_HEADS, _sigmoid(p2), g)


def gdn_prep(p1, p2, conv_w, a_log, dt_bias, layer, seq, tm):
    t = p1.shape[0]
    tm = min(tm, seq)
    w = GDN_HEADS * GDN_DK
    cblk = GQ_OFF // w
    hb = tm // SUBLANES

    def xspec(r):
        return pl.BlockSpec((tm, w), lambda i: (i, cblk + r))

    def hspec(r):
        return pl.BlockSpec((SUBLANES, w), lambda i: (jnp.maximum(i * hb - 1, 0), cblk + r))

    def cwspec(r):
        return pl.BlockSpec((None, GDN_CONV, w), lambda i: (layer, 0, r))

    pad = jnp.zeros((LANES - 2 * GDN_HEADS,), F32)
    z8 = jnp.zeros((GDN_HEADS,), F32)
    alog_row = jnp.concatenate([z8, a_log[layer], pad])[None, :]
    dtb_row = jnp.concatenate([z8, dt_bias[layer], pad])[None, :]
    return pl.pallas_call(
        functools.partial(_gdn_prep_kernel, tiles_per_seq=seq // tm),
        grid=(t // tm,),
        in_specs=[xspec(0), xspec(1), xspec(2), hspec(0), hspec(1), hspec(2),
                  cwspec(0), cwspec(1), cwspec(2),
                  pl.BlockSpec((tm, LANES), lambda i: (i, 2)),
                  pl.BlockSpec((1, LANES), lambda i: (0, 0)),
                  pl.BlockSpec((1, LANES), lambda i: (0, 0))],
        out_specs=[pl.BlockSpec((tm, w), lambda i: (i, 0))] * 3 + [pl.BlockSpec((tm, LANES), lambda i: (i, 0))],
        out_shape=[jax.ShapeDtypeStruct((t, w), BF16)] * 3 + [jax.ShapeDtypeStruct((t, LANES), F32)],
        compiler_params=_cparams("arbitrary"),
        name="gdn_prep",
    )(p1, p1, p1, p1, p1, p1, conv_w, conv_w, conv_w, p2, alog_row, dtb_row)


def _gdn_chunk_kernel(q_ref, k_ref, v_ref, z_ref, bg_ref, nw_ref, o_ref, st_ref, uw_scr, qk_scr, qg_scr, kg_scr):
    nb, tb, _ = q_ref.shape
    c = CHUNK
    nck = tb // c

    @pl.when(pl.program_id(0) == 0)
    def _():
        st_ref[...] = jnp.zeros(st_ref.shape, F32)

    r = _iota((tb, tb), 0)
    cc = _iota((tb, tb), 1)
    same = (r // c) == (cc // c)
    incl = same & (r >= cc)
    strict = same & (r > cc)
    units = [(b, h) for b in range(nb) for h in range(GDN_HEADS)]
    lmat = jnp.where(incl, 1.0, 0.0).astype(BF16)
    g_cum = [_dot_exact_lhs(lmat, bg_ref[b], 3) for b in range(nb)]
    g_tot = [jnp.concatenate([jnp.broadcast_to(g[(ci + 1) * c - 1:(ci + 1) * c, :], (c, LANES))
                              for ci in range(nck)], axis=0) for g in g_cum]
    g_cum_t = [g.T for g in g_cum]

    def cols(h):
        return slice(h * LANES, (h + 1) * LANES)

    def glane(h):
        return slice(GDN_HEADS + h, GDN_HEADS + h + 1)

    for g0 in range(0, len(units), GROUP):
        grp = units[g0:g0 + GROUP]
        ks = [k_ref[b, :, cols(h)] for b, h in grp]
        gcs = [g_cum[b][:, glane(h)] for b, h in grp]
        betas = [bg_ref[b, :, h:h + 1] for b, h in grp]
        decs = [jnp.exp(jnp.where(incl, gc - g_cum_t[b][glane(h), :], -jnp.inf))
                for gc, (b, h) in zip(gcs, grp)]
        a_s = [jnp.where(strict, _dot_nt(k, k) * dec, 0.0) * beta for k, dec, beta in zip(ks, decs, betas)]
        ns = _tri_inv_m1(a_s, c)
        for idx, (b, h) in enumerate(grp):
            u = g0 + idx
            q = q_ref[b, :, cols(h)]
            kf = ks[idx].astype(F32)
            vf = v_ref[b, :, cols(h)].astype(F32)
            eg = jnp.exp(gcs[idx])
            rhs = jnp.concatenate([betas[idx] * vf, (betas[idx] * eg) * kf], axis=1)
            uw_scr[u] = rhs + _dot(ns[idx], rhs)
            qk_scr[u] = (_dot_nt(q, ks[idx]) * decs[idx]).astype(BF16)
            qg_scr[u] = (q.astype(F32) * eg).astype(BF16)
            kg_scr[u] = (kf * jnp.exp(g_tot[b][:, glane(h)] - gcs[idx])).astype(BF16)

    states = [st_ref[u] for u in range(len(units))]
    w_parts = [[] for _ in units]
    og_parts = [[] for _ in units]
    for ci in range(nck):
        rows = slice(ci * c, (ci + 1) * c)
        sbs = [s.astype(BF16) for s in states]
        wqs = [_dot(jnp.concatenate([uw_scr[u, rows, GDN_DV:].astype(BF16), qg_scr[u, rows, :]], axis=0), sbs[u])
               for u in range(len(units))]
        for u in range(len(units)):
            w_parts[u].append((uw_scr[u, rows, :GDN_DV] - wqs[u][:c]).astype(BF16))
            og_parts[u].append(wqs[u][c:])
        states = [jnp.exp(g_tot[b][ci * c:ci * c + 1, glane(h)]) * states[u]
                  + _dot_tn(kg_scr[u, rows, :], w_parts[u][ci])
                  for u, (b, h) in enumerate(units)]

    nw = nw_ref[...]
    for u, (b, h) in enumerate(units):
        o = jnp.concatenate(og_parts[u], axis=0) + _dot(qk_scr[u], jnp.concatenate(w_parts[u], axis=0))
        zf = z_ref[b, :, cols(h)].astype(F32)
        o_ref[b, :, cols(h)] = (_rms_rows(o, nw) * (zf * _sigmoid(zf))).astype(o_ref.dtype)
        st_ref[u] = states[u]


def gdn_chunk(qn, kn, vc, p1, bg, norm_w, layer, nb, seq):
    t, w = qn.shape
    tb = min(GDN_STEP, seq)
    zblk = (GQ_OFF + 3 * w) // w
    nu = nb * GDN_HEADS

    def xspec(cb):
        return pl.BlockSpec((nb, tb, w), lambda s: (0, s, cb))

    out = pl.pallas_call(
        _gdn_chunk_kernel,
        grid=(seq // tb,),
        in_specs=[xspec(0), xspec(0), xspec(0), xspec(zblk),
                  pl.BlockSpec((nb, tb, LANES), lambda s: (0, s, 0)),
                  pl.BlockSpec((None, 1, GDN_DV), lambda s: (layer, 0, 0))],
        out_specs=xspec(0),
        out_shape=jax.ShapeDtypeStruct((nb, seq, w), BF16),
        scratch_shapes=[pltpu.VMEM((nu, GDN_DK, GDN_DV), F32),
                        pltpu.VMEM((nu, tb, 2 * GDN_DV), F32),
                        pltpu.VMEM((nu, tb, tb), BF16),
                        pltpu.VMEM((nu, tb, GDN_DK), BF16),
                        pltpu.VMEM((nu, tb, GDN_DK), BF16)],
        compiler_params=_cparams("arbitrary"),
        name="gdn_chunk",
    )(qn.reshape(nb, seq, w), kn.reshape(nb, seq, w), vc.reshape(nb, seq, w),
      p1.reshape(nb, seq, p1.shape[-1]), bg.reshape(nb, seq, LANES), norm_w)
    return out.reshape(t, w)


def _head_sum_matrix():
    return ((_iota((LANES, LANES), 0) // RWKV_HEAD) == (_iota((LANES, LANES), 1) // RWKV_HEAD)).astype(BF16)


def _rwkv_prep_kernel(r_ref, k_ref, v_ref, x_ref, rh_ref, kh_ref, vh_ref, xh_ref,
                      mur_ref, muk_ref, muv_ref, mux_ref, w0_ref, wup_ref, a0_ref, aup_ref, gup_ref,
                      kk_w_ref, ka_ref, rk_ref,
                      kkd_ref, rt_ref, kt_ref, bt_ref, kh_o_ref, bh_o_ref, v_o_ref, pc_ref, bonus_ref, g_ref,
                      *, tiles_per_seq):
    first = (pl.program_id(0) % tiles_per_seq) == 0
    c = CHUNK
    tm = r_ref.shape[0]

    def shift_mix(x_r, h_r, mu_r):
        x = x_r[...].astype(F32)
        halo = jnp.where(first, 0.0, h_r[...].astype(F32))
        prev = _shifted_rows(x, halo, 1)
        return x + (prev - x) * mu_r[...]

    r = shift_mix(r_ref, rh_ref, mur_ref)
    k = shift_mix(k_ref, kh_ref, muk_ref)
    v = shift_mix(v_ref, vh_ref, muv_ref)
    xs = shift_mix(x_ref, xh_ref, mux_ref)
    xw = xs[:, 0:LANES]
    lane = _iota(xw.shape, 1)
    tw = jnp.where(lane < RWKV_W_RANK, jnp.tanh(xw), 0.0)
    xa = jnp.where(lane >= RWKV_W_RANK, xw, 0.0)
    sg = _sigmoid(xs[:, LANES:2 * LANES])
    w_log = -_softplus(-(w0_ref[...] + _dot(tw, wup_ref[...]))) - 0.5
    lw = -jnp.exp(w_log)
    a = _sigmoid(a0_ref[...] + _dot(xa, aup_ref[...]))
    g = _dot(sg, gup_ref[...])
    hs = _head_sum_matrix()
    kkw = k * kk_w_ref[...]
    k2 = k * (1.0 + (a - 1.0) * ka_ref[...])
    rkr = r * k2 * rk_ref[...]
    kk_parts = []
    bonus_parts = []
    for blk in range(RWKV_WIDTH // LANES):
        cols = slice(blk * LANES, (blk + 1) * LANES)
        x_blk = kkw[:, cols]
        ss = _dot_exact_rhs(x_blk * x_blk, hs, 2)
        kk_parts.append(x_blk * lax.rsqrt(ss + 1e-6))
        bonus_parts.append(_dot_exact_rhs(rkr[:, cols], hs, 2) * v[:, cols])
    kk = jnp.concatenate(kk_parts, axis=1)
    bonus_ref[...] = jnp.concatenate(bonus_parts, axis=1).astype(bonus_ref.dtype)
    g_ref[...] = g.astype(g_ref.dtype)
    b = kk * a
    ri = _iota((tm, tm), 0)
    cj = _iota((tm, tm), 1)
    same = (ri // c) == (cj // c)
    gi = _dot_exact_lhs(jnp.where(same & (ri >= cj), 1.0, 0.0).astype(BF16), lw, 3)
    ge = gi - lw
    e_neg = jnp.exp(-gi)
    kkd_ref[...] = (kk * jnp.exp(ge)).astype(BF16)
    rt_ref[...] = (r * jnp.exp(gi)).astype(BF16)
    kt_ref[...] = (k2 * e_neg).astype(BF16)
    bt_ref[...] = (b * e_neg).astype(BF16)
    v_o_ref[...] = v.astype(BF16)
    for ci in range(tm // c):
        rows = slice(ci * c, (ci + 1) * c)
        egl = jnp.exp(gi[(ci + 1) * c - 1:(ci + 1) * c, :])
        pc_ref[ci] = egl
        e_tail = egl * e_neg[rows]
        kh_o_ref[rows, :] = (k2[rows] * e_tail).astype(BF16)
        bh_o_ref[rows, :] = (b[rows] * e_tail).astype(BF16)


def rwkv_prep(p1, p2, mu, w0, w_up, a0, a_up, g_up, k_k, k_a, r_k, layer, seq, tm):
    t = p1.shape[0]
    tm = min(tm, seq)
    w = RWKV_WIDTH
    cblk = RR_OFF // w
    hb = tm // SUBLANES
    xw2 = 2 * LANES

    def xspec(r):
        return pl.BlockSpec((tm, w), lambda i: (i, cblk + r))

    def hspec(r):
        return pl.BlockSpec((SUBLANES, w), lambda i: (jnp.maximum(i * hb - 1, 0), cblk + r))

    def row(width, blk):
        return pl.BlockSpec((None, 1, width), lambda i: (layer, 0, blk))

    zpad = jnp.zeros((w_up.shape[0], RWKV_W_RANK, w), F32)
    wup_p = jnp.concatenate([w_up, zpad], axis=1).astype(BF16)
    aup_p = jnp.concatenate([zpad, a_up], axis=1).astype(BF16)
    gup_b = g_up.astype(BF16)
    mu3 = mu[:, None, :]
    rk_row = r_k.reshape(r_k.shape[0], 1, w)
    outs = [jax.ShapeDtypeStruct((t, w), BF16)] * 7
    outs += [jax.ShapeDtypeStruct((t // CHUNK, 1, w), F32),
             jax.ShapeDtypeStruct((t, w), BF16), jax.ShapeDtypeStruct((t, w), BF16)]
    ospec = pl.BlockSpec((tm, w), lambda i: (i, 0))
    out_specs = [ospec] * 7 + [pl.BlockSpec((tm // CHUNK, 1, w), lambda i: (i, 0, 0)), ospec, ospec]
    return pl.pallas_call(
        functools.partial(_rwkv_prep_kernel, tiles_per_seq=seq // tm),
        grid=(t // tm,),
        in_specs=[xspec(0), xspec(1), xspec(2), pl.BlockSpec((tm, xw2), lambda i: (i, 0)),
                  hspec(0), hspec(1), hspec(2),
                  pl.BlockSpec((SUBLANES, xw2), lambda i: (jnp.maximum(i * hb - 1, 0), 0)),
                  row(w, 0), row(w, 1), row(w, 2),
                  pl.BlockSpec((None, 1, xw2), lambda i: (layer, 0, 3 * w // xw2)),
                  row(w, 0),
                  pl.BlockSpec((None, LANES, w), lambda i: (layer, 0, 0)),
                  row(w, 0),
                  pl.BlockSpec((None, LANES, w), lambda i: (layer, 0, 0)),
                  pl.BlockSpec((None, RWKV_G_RANK, w), lambda i: (layer, 0, 0)),
                  row(w, 0), row(w, 0), row(w, 0)],
        out_specs=out_specs,
        out_shape=outs,
        compiler_params=_cparams("arbitrary"),
        name="rwkv_prep",
    )(p1, p1, p1, p2, p1, p1, p1, p2, mu3, mu3, mu3, mu3, w0[:, None, :], wup_p, a0[:, None, :], aup_p,
      gup_b, k_k[:, None, :], k_a[:, None, :], rk_row)


def _rwkv_chunk_kernel(kkd_ref, rt_ref, kt_ref, bt_ref, kh_ref, bh_ref, v_ref, pc_ref, y_ref, st_ref,
                       n_scr, arb_scr, uloc_scr, yloc_scr):
    nb, tb, w = kkd_ref.shape
    c = CHUNK
    nck = tb // c
    n2 = 2 * tb
    npair = w // LANES

    @pl.when(pl.program_id(0) == 0)
    def _():
        st_ref[...] = jnp.zeros(st_ref.shape, F32)

    r = _iota((n2, n2), 0)
    cc = _iota((n2, n2), 1)
    same = (r // c) == (cc // c)
    incl = same & (r >= cc)
    strict = same & (r > cc)
    head0 = _iota((1, LANES), 1) < RWKV_HEAD
    bd = (_iota((LANES, LANES), 0) // RWKV_HEAD) == (_iota((LANES, LANES), 1) // RWKV_HEAD)
    units = [(b, p) for b in range(nb) for p in range(npair)]

    def cols(p):
        return slice(p * LANES, (p + 1) * LANES)

    def rows(ci):
        return slice(ci * c, (ci + 1) * c)

    def blk(ci):
        return slice(ci * LANES, (ci + 1) * LANES)

    def stack_heads(x):
        x0 = jnp.where(head0, x, jnp.zeros_like(x))
        x1 = jnp.where(head0, jnp.zeros_like(x), x)
        return jnp.concatenate([piece[rows(ci)] for ci in range(nck) for piece in (x0, x1)], axis=0)

    def stack_dup(x):
        return jnp.concatenate([x[rows(ci)] for ci in range(nck) for _ in range(2)], axis=0)

    for g0 in range(0, len(units), GROUP):
        grp = units[g0:g0 + GROUP]
        lks = [stack_heads(kkd_ref[b, :, cols(p)]) for b, p in grp]
        lrs = [stack_heads(rt_ref[b, :, cols(p)]) for b, p in grp]
        rks = [stack_dup(kt_ref[b, :, cols(p)]) for b, p in grp]
        rbs = [stack_dup(bt_ref[b, :, cols(p)]) for b, p in grp]
        v2s = [stack_dup(v_ref[b, :, cols(p)]) for b, p in grp]
        a_kb = [jnp.where(strict, _dot_nt(lk, rb), 0.0) for lk, rb in zip(lks, rbs)]
        a_kk = [jnp.where(strict, _dot_nt(lk, rk), 0.0) for lk, rk in zip(lks, rks)]
        ns = _tri_inv_m1(a_kb, c)
        avs = [_dot(a, v2) for a, v2 in zip(a_kk, v2s)]
        for idx in range(len(grp)):
            u = g0 + idx
            nbf = ns[idx].astype(BF16)
            n_scr[u] = nbf
            uloc_scr[u] = avs[idx] + _dot(nbf, avs[idx])
            a_rk = jnp.where(incl, _dot_nt(lrs[idx], rks[idx]), 0.0)
            yloc_scr[u] = _dot(a_rk, v2s[idx])
            arb_scr[u] = jnp.where(incl, _dot_nt(lrs[idx], rbs[idx]), 0.0).astype(BF16)

    nu = len(units)
    states = [st_ref[u] for u in range(nu)]
    rs_parts = [[] for _ in units]
    ub_parts = [[] for _ in units]
    for ci in range(nck):
        sbs = [s.astype(BF16) for s in states]
        kss = [_dot_nt(jnp.concatenate([kkd_ref[b, rows(ci), cols(p)], rt_ref[b, rows(ci), cols(p)]], axis=0), sbs[u])
               for u, (b, p) in enumerate(units)]
        ks2 = [jnp.concatenate([ks[:c], ks[:c]], axis=0) for ks in kss]
        tks = [ks2[u] + _dot(n_scr[u, blk(ci), blk(ci)], ks2[u]) for u in range(nu)]
        for u in range(nu):
            u2 = uloc_scr[u, blk(ci), :] + tks[u]
            ub_parts[u].append(jnp.where(head0, u2[:c], u2[c:]).astype(BF16))
            rs_parts[u].append(kss[u][c:])
        states = [states[u] * pc_ref[b, ci, :, cols(p)]
                  + jnp.where(bd, _dot_tn(jnp.concatenate([v_ref[b, rows(ci), cols(p)], ub_parts[u][ci]], axis=0),
                                          jnp.concatenate([kh_ref[b, rows(ci), cols(p)],
                                                           -bh_ref[b, rows(ci), cols(p)]], axis=0)), 0.0)
                  for u, (b, p) in enumerate(units)]

    for u, (b, p) in enumerate(units):
        u2all = jnp.concatenate([ub_parts[u][ci] for ci in range(nck) for _ in range(2)], axis=0)
        yall = yloc_scr[u] - _dot(arb_scr[u], u2all)
        for ci in range(nck):
            yb = yall[blk(ci), :]
            y_ref[b, rows(ci), cols(p)] = rs_parts[u][ci] + jnp.where(head0, yb[:c], yb[c:])
        st_ref[u] = states[u]


def rwkv_chunk(tensors, pc, nb, seq):
    t, w = tensors[0].shape
    tb = min(RWKV_STEP, seq)
    nck = tb // CHUNK
    nu = nb * (w // LANES)
    xspec = pl.BlockSpec((nb, tb, w), lambda s: (0, s, 0))
    out = pl.pallas_call(
        _rwkv_chunk_kernel,
        grid=(seq // tb,),
        in_specs=[xspec] * 7 + [pl.BlockSpec((nb, nck, 1, w), lambda s: (0, s, 0, 0))],
        out_specs=xspec,
        out_shape=jax.ShapeDtypeStruct((nb, seq, w), F32),
        scratch_shapes=[pltpu.VMEM((nu, LANES, LANES), F32),
                        pltpu.VMEM((nu, 2 * tb, 2 * tb), BF16),
                        pltpu.VMEM((nu, 2 * tb, 2 * tb), BF16),
                        pltpu.VMEM((nu, 2 * tb, LANES), F32),
                        pltpu.VMEM((nu, 2 * tb, LANES), F32)],
        compiler_params=_cparams("arbitrary"),
        name="rwkv_chunk",
    )(*[x.reshape(nb, seq, w) for x in tensors], pc.reshape(nb, seq // CHUNK, 1, w))
    return out.reshape(t, w)


def _rwkv_post_kernel(y_ref, bonus_ref, g_ref, lw_ref, lb_ref, o_ref):
    hm = _head_sum_matrix() * (1.0 / RWKV_HEAD)
    hm = hm.astype(BF16)
    for blk in range(RWKV_WIDTH // LANES):
        cols = slice(blk * LANES, (blk + 1) * LANES)
        y = y_ref[:, cols]
        mean = _dot_exact_rhs(y, hm, 3)
        yc = y - mean
        var = _dot_exact_rhs(yc * yc, hm, 3)
        yn = yc * lax.rsqrt(var + RWKV_LN_EPS) * lw_ref[:, cols] + lb_ref[:, cols]
        out = (yn + bonus_ref[:, cols].astype(F32)) * g_ref[:, cols].astype(F32)
        o_ref[:, cols] = out.astype(o_ref.dtype)


def rwkv_post(y, bonus, g, lnx_w, lnx_b, layer, tm):
    t, w = y.shape
    tm = min(tm, t)
    xspec = pl.BlockSpec((tm, w), lambda i: (i, 0))
    rspec = pl.BlockSpec((None, 1, w), lambda i: (layer, 0, 0))
    return pl.pallas_call(
        _rwkv_post_kernel,
        grid=(t // tm,),
        in_specs=[xspec, xspec, xspec, rspec, rspec],
        out_specs=xspec,
        out_shape=jax.ShapeDtypeStruct((t, w), BF16),
        compiler_params=_cparams("arbitrary"),
        name="rwkv_post",
    )(y, bonus, g, lnx_w[:, None, :], lnx_b[:, None, :])


def _relayout_w_in(w_in):
    nl, d, _ = w_in.shape
    wt = jnp.swapaxes(w_in, 1, 2).astype(BF16)

    def rows(lo, n):
        return wt[:, lo:lo + n, :]

    def zeros(n):
        return jnp.zeros((nl, n, d), BF16)

    kr = rows(_O_KR, MLA_ROPE)
    half = MLA_ROPE // 2
    kr_swap = jnp.concatenate([-kr[:, half:, :], kr[:, :half, :]], axis=1)
    w1 = jnp.concatenate([
        rows(_O_CQ, MLA_Q_RANK),
        kr, zeros(LANES - MLA_ROPE), kr_swap, zeros(LANES - MLA_ROPE),
        rows(_O_CKV, MLA_KV_RANK), zeros(GQ_OFF - CKV_OFF - MLA_KV_RANK),
        rows(_O_GDN, 4096),
        rows(_O_RWKV, 3072),
        rows(_O_GATE, 3 * D_MODEL)], axis=1)
    w2 = jnp.concatenate([rows(_O_XW, 256), rows(_O_GB, 16), zeros(NP2 - 272)], axis=1)
    return w1, w2


def _relayout_w_uq(w_uq):
    nl, r, _ = w_uq.shape
    wq = w_uq.reshape(nl, r, MLA_HEADS, MLA_NOPE + MLA_ROPE)
    nope = wq[..., :MLA_NOPE]
    rope = wq[..., MLA_NOPE:]
    half = MLA_ROPE // 2
    swap = jnp.concatenate([-rope[..., half:], rope[..., :half]], axis=-1)
    z = jnp.zeros(rope.shape[:-1] + (LANES - MLA_ROPE,), w_uq.dtype)
    return jnp.concatenate([nope, rope, z, swap, z], axis=-1).reshape(nl, r, 3 * MLA_HEADS * LANES).astype(BF16)


def _relayout_w_ukv(w_ukv):
    nl, r, _ = w_ukv.shape
    wkv = w_ukv.reshape(nl, r, MLA_HEADS, MLA_NOPE + MLA_V)
    wk = wkv[..., :MLA_NOPE].reshape(nl, r, MLA_HEADS * MLA_NOPE)
    wv = wkv[..., MLA_NOPE:].reshape(nl, r, MLA_HEADS * MLA_V)
    return jnp.concatenate([wk, wv], axis=-1).astype(BF16)


def _rope_table(positions):
    inv = 1.0 / (ROPE_THETA ** (jnp.arange(0, MLA_ROPE, 2, dtype=F32) / MLA_ROPE))
    ang = positions.astype(F32)[..., None] * inv
    cos, sin = jnp.cos(ang), jnp.sin(ang)
    z = jnp.zeros(cos.shape[:-1] + (LANES - MLA_ROPE,), F32)
    cs = jnp.concatenate([cos, cos, z, sin, sin, z], axis=-1)
    return cs.reshape(-1, 2 * LANES)


def kernel(x, c, positions, w_ada, b_ada, norm1_w, w_in, mla_q_norm_w, mla_w_uq, mla_kv_norm_w, mla_w_ukv,
           gdn_conv_w, gdn_a_log, gdn_dt_bias, gdn_norm_w, rwkv_mu, rwkv_w0, rwkv_w_up, rwkv_a0, rwkv_a_up,
           rwkv_g_up, rwkv_k_k, rwkv_k_a, rwkv_r_k, rwkv_lnx_w, rwkv_lnx_b, w_branch, w_out, norm2_w,
           w_gate_up, w_down, final_norm_w):
    nb, seq, d = x.shape
    nl = w_in.shape[0]
    assert seq % (2 * CHUNK) == 0 and d == D_MODEL

    w1, w2 = _relayout_w_in(w_in)
    wq3 = _relayout_w_uq(mla_w_uq)
    wkv = _relayout_w_ukv(mla_w_ukv)
    w_branch_b = w_branch.astype(BF16)
    w_out_b = w_out.astype(BF16)
    w_gu_b = w_gate_up.astype(BF16)
    w_down_b = w_down.astype(BF16)
    cs = _rope_table(positions)

    mod = adaln_mod(c, w_ada, b_ada)
    modr = mod.reshape(nl * nb * 6, 1, d)
    n1 = norm1_w[:, None, :]
    n2 = norm2_w[:, None, :]
    qnw = mla_q_norm_w[:, None, :]
    kvnw = mla_kv_norm_w[:, None, :]
    gnw = gdn_norm_w[:, None, :]

    xf = x.reshape(nb * seq, d)
    for l in range(nl):
        p1, p2 = norm_gemm(xf, n1, modr, w1, w2, l, 0, nb, seq, *GEMM_TILES["w_in"])
        q, k, vt = mla_proj(p1, cs, qnw, kvnw, wq3, wkv, l, nb, seq, ROW_TILE)
        o_a = flash_attn(q, k, vt, nb, seq, ATTN_TILE)
        qn, kn, vc, bg = gdn_prep(p1, p2, gdn_conv_w, gdn_a_log, gdn_dt_bias, l, seq, ROW_TILE)
        o_b = gdn_chunk(qn, kn, vc, p1, bg, gnw, l, nb, seq)
        prep = rwkv_prep(p1, p2, rwkv_mu, rwkv_w0, rwkv_w_up, rwkv_a0, rwkv_a_up, rwkv_g_up,
                         rwkv_k_k, rwkv_k_a, rwkv_r_k, l, seq, ROW_TILE)
        y = rwkv_chunk(prep[:7], prep[7], nb, seq)
        o_c = rwkv_post(y, prep[8], prep[9], rwkv_lnx_w, rwkv_lnx_b, l, ROW_TILE)
        merged = merge_gemm(o_a, o_b, o_c, w_branch_b, p1, l, *GEMM_TILES["merge"])
        xf = resid_gemm(merged, w_out_b, xf, modr, l, 2, nb, seq, *GEMM_TILES["w_out"])
        act = ffn_up(xf, n2, modr, w_gu_b, l, nb, seq, *GEMM_TILES["ffn_up"])
        xf = resid_gemm(act, w_down_b, xf, modr, l, 5, nb, seq, *GEMM_TILES["ffn_down"])
    return final_norm(xf, final_norm_w, ROW_TILE).reshape(nb, seq, d)
```

```python
import functools
import math

import jax
import jax.numpy as jnp
from jax import lax
from jax.experimental import pallas as pl
from jax.experimental.pallas import tpu as pltpu

F32 = jnp.float32
BF16 = jnp.bfloat16

D_MODEL = 2048
MLA_HEADS = 8
MLA_Q_RANK = 768
MLA_KV_RANK = 512
MLA_NOPE = 128
MLA_ROPE = 64
MLA_V = 128
ROPE_THETA = 10000.0
GDN_HEADS = 8
GDN_DK = 128
GDN_DV = 128
GDN_CONV = 4
RWKV_HEADS = 16
RWKV_HEAD = 64
RWKV_WIDTH = RWKV_HEADS * RWKV_HEAD
RWKV_W_RANK = 64
RWKV_A_RANK = 64
RWKV_G_RANK = 128
RWKV_LN_EPS = 64e-5
D_FF = 5632
NORM_EPS = 1e-6
CHUNK = 64
MXU_TILE = 256
GDN_STEP = MXU_TILE
RWKV_STEP = MXU_TILE // 2
GROUP = 4
FLASH_HEADS = 8
VT_PAD = 16
LANES = 128
SUBLANES = 8
VMEM_LIMIT = 56 * 1024 * 1024

CQ_OFF = 0
KR_OFF = 768
CKV_OFF = 1024
GQ_OFF = 2048
RR_OFF = 6144
GATE_OFF = 9216
NP1 = 15360
NP2 = 384

GEMM_TILES = {
    "w_in": (1024, 1536),
    "merge": (1024, 1024),
    "w_out": (1024, 1024),
    "ffn_up": (1024, 512),
    "ffn_down": (1024, 512),
}
ROW_TILE = 512
ATTN_TILE = 512

_O_CQ = 0
_O_CKV = 768
_O_KR = 1280
_O_GDN = 1344
_O_GZ = _O_GDN + 3072
_O_GB = _O_GZ + 1024
_O_GA = _O_GB + 8
_O_RWKV = _O_GA + 8
_O_XW = _O_RWKV + 3072
_O_XA = _O_XW + 64
_O_XG = _O_XA + 64
_O_GATE = _O_XG + 128


def _cparams(*sem):
    return pltpu.CompilerParams(dimension_semantics=sem, vmem_limit_bytes=VMEM_LIMIT)


def _dot(a, b):
    return jnp.dot(a.astype(BF16), b.astype(BF16), preferred_element_type=F32)


def _dot_nt(a, b):
    return lax.dot_general(a.astype(BF16), b.astype(BF16), (((1,), (1,)), ((), ())),
                           preferred_element_type=F32)


def _dot_tn(a, b):
    return lax.dot_general(a.astype(BF16), b.astype(BF16), (((0,), (0,)), ((), ())),
                           preferred_element_type=F32)


def _split(a, terms):
    parts = []
    rem = a
    for _ in range(terms):
        p = rem.astype(BF16)
        parts.append(p)
        rem = rem - p.astype(F32)
    return parts


def _dot_exact_lhs(a_bf, b, terms):
    out = None
    for p in _split(b, terms):
        t = _dot(a_bf, p)
        out = t if out is None else out + t
    return out


def _dot_exact_rhs(a, b_bf, terms):
    out = None
    for p in _split(a, terms):
        t = _dot(p, b_bf)
        out = t if out is None else out + t
    return out


def _iota(shape, dim):
    return lax.broadcasted_iota(jnp.int32, shape, dim)


def _tri_inv_m1(a_list, n):
    m = a_list[0].shape[0]
    nblk = m // n
    lane_blk = _iota((n, m), 1) // n

    def compact(x):
        out = x[0:n]
        for i in range(1, nblk):
            out = out + x[i * n:(i + 1) * n]
        return out

    def expand(xc):
        zero = jnp.zeros_like(xc)
        return jnp.concatenate([jnp.where(lane_blk == i, xc, zero) for i in range(nblk)], axis=0)

    a_bf = [a.astype(BF16) for a in a_list]
    acs = [compact(a) for a in a_list]
    ns = [-ac for ac in acs]
    ps = [_dot(ac, ab) for ac, ab in zip(acs, a_bf)]
    k = 2
    while True:
        pbds = [expand(p.astype(BF16)) for p in ps]
        both = [_dot(jnp.concatenate([nn, p], axis=0), pbd) for nn, p, pbd in zip(ns, ps, pbds)]
        ns = [nn + p + bo[0:n] for nn, p, bo in zip(ns, ps, both)]
        k *= 2
        if k >= n:
            break
        ps = [bo[n:2 * n] for bo in both]
    return [expand(nn) for nn in ns]


def _sigmoid(x):
    return jax.nn.sigmoid(x)


def _softplus(x):
    return jnp.maximum(x, 0.0) + jnp.log1p(jnp.exp(-jnp.abs(x)))


def _mod_kernel(c_ref, w_ref, b_ref, o_ref):
    w = w_ref[...]
    for m in range(c_ref.shape[0]):
        c = c_ref[m]
        ca = c * _sigmoid(c)
        o_ref[m:m + 1, :] = jnp.sum(ca * w, axis=0, keepdims=True) + b_ref[...]


def adaln_mod(c, w_ada, b_ada):
    nl, d, n = w_ada.shape
    b = c.shape[0]
    tn = min(1024, n)
    return pl.pallas_call(
        _mod_kernel,
        grid=(nl, n // tn),
        in_specs=[pl.BlockSpec((b, d, 1), lambda l, j: (0, 0, 0)),
                  pl.BlockSpec((None, d, tn), lambda l, j: (l, 0, j)),
                  pl.BlockSpec((None, 1, tn), lambda l, j: (l, 0, j))],
        out_specs=pl.BlockSpec((None, b, tn), lambda l, j: (l, 0, j)),
        out_shape=jax.ShapeDtypeStruct((nl, b, n), F32),
        compiler_params=_cparams("arbitrary", "arbitrary"),
        name="adaln_mod",
    )(c[:, :, None], w_ada, b_ada[:, None, :])


def _modulate_to(h_scr, x_ref, nw_ref, sh_ref, sc_ref):
    x = x_ref[...]
    ms = jnp.mean(x * x, axis=-1, keepdims=True)
    y = x * lax.rsqrt(ms + NORM_EPS) * nw_ref[...]
    h_scr[...] = (y * (1.0 + sc_ref[...]) + sh_ref[...]).astype(BF16)


def _norm_gemm_kernel(x_ref, nw_ref, sh_ref, sc_ref, w_ref, w2_ref, o_ref, o2_ref, h_scr):
    @pl.when(pl.program_id(1) == 0)
    def _():
        _modulate_to(h_scr, x_ref, nw_ref, sh_ref, sc_ref)
        o2_ref[...] = _dot_nt(h_scr[...], w2_ref[...])

    o_ref[...] = _dot_nt(h_scr[...], w_ref[...]).astype(o_ref.dtype)


def _mod_spec(d, layer, nb, which, tiles_per_seq):
    return pl.BlockSpec((None, 1, d),
                        lambda i, j: ((layer * nb + i // tiles_per_seq) * 6 + which, 0, 0))


def norm_gemm(x, norm_w, modr, w, w2, layer, which_sh, nb, seq, tm, tn):
    t, d = x.shape
    n = w.shape[1]
    n2 = w2.shape[1]
    tm = min(tm, seq)
    tn = min(tn, n)
    tps = seq // tm
    return pl.pallas_call(
        _norm_gemm_kernel,
        grid=(t // tm, n // tn),
        in_specs=[pl.BlockSpec((tm, d), lambda i, j: (i, 0)),
                  pl.BlockSpec((None, 1, d), lambda i, j: (layer, 0, 0)),
                  _mod_spec(d, layer, nb, which_sh, tps),
                  _mod_spec(d, layer, nb, which_sh + 1, tps),
                  pl.BlockSpec((None, tn, d), lambda i, j: (layer, j, 0)),
                  pl.BlockSpec((None, n2, d), lambda i, j: (layer, 0, 0))],
        out_specs=[pl.BlockSpec((tm, tn), lambda i, j: (i, j)),
                   pl.BlockSpec((tm, n2), lambda i, j: (i, 0))],
        out_shape=[jax.ShapeDtypeStruct((t, n), BF16), jax.ShapeDtypeStruct((t, n2), F32)],
        scratch_shapes=[pltpu.VMEM((tm, d), BF16)],
        compiler_params=_cparams("arbitrary", "arbitrary"),
        name="norm_gemm",
    )(x, norm_w, modr, modr, w, w2)


def _ffn_up_kernel(x_ref, nw_ref, sh_ref, sc_ref, wg_ref, wu_ref, o_ref, h_scr):
    @pl.when(pl.program_id(1) == 0)
    def _():
        _modulate_to(h_scr, x_ref, nw_ref, sh_ref, sc_ref)

    h = h_scr[...]
    gate = jnp.dot(h, wg_ref[...], preferred_element_type=F32)
    up = jnp.dot(h, wu_ref[...], preferred_element_type=F32)
    o_ref[...] = (gate * _sigmoid(gate) * up).astype(o_ref.dtype)


def ffn_up(x, norm_w, modr, w_gu, layer, nb, seq, tm, tn):
    t, d = x.shape
    f = w_gu.shape[-1] // 2
    tm = min(tm, seq)
    tn = min(tn, f)
    tps = seq // tm
    nf = f // tn
    return pl.pallas_call(
        _ffn_up_kernel,
        grid=(t // tm, nf),
        in_specs=[pl.BlockSpec((tm, d), lambda i, j: (i, 0)),
                  pl.BlockSpec((None, 1, d), lambda i, j: (layer, 0, 0)),
                  _mod_spec(d, layer, nb, 3, tps),
                  _mod_spec(d, layer, nb, 4, tps),
                  pl.BlockSpec((None, d, tn), lambda i, j: (layer, 0, j)),
                  pl.BlockSpec((None, d, tn), lambda i, j: (layer, 0, j + nf))],
        out_specs=pl.BlockSpec((tm, tn), lambda i, j: (i, j)),
        out_shape=jax.ShapeDtypeStruct((t, f), BF16),
        scratch_shapes=[pltpu.VMEM((tm, d), BF16)],
        compiler_params=_cparams("arbitrary", "arbitrary"),
        name="ffn_up",
    )(x, norm_w, modr, modr, w_gu, w_gu)


def _resid_gemm_kernel(a_ref, w_ref, x_ref, gt_ref, o_ref):
    y = jnp.dot(a_ref[...], w_ref[...], preferred_element_type=F32)
    o_ref[...] = x_ref[...] + gt_ref[...] * y


def resid_gemm(a, w, x, modr, layer, which_gt, nb, seq, tm, tn):
    t, k = a.shape
    d = x.shape[-1]
    tm = min(tm, seq)
    tn = min(tn, d)
    tps = seq // tm
    return pl.pallas_call(
        _resid_gemm_kernel,
        grid=(t // tm, d // tn),
        in_specs=[pl.BlockSpec((tm, k), lambda i, j: (i, 0)),
                  pl.BlockSpec((None, k, tn), lambda i, j: (layer, 0, j)),
                  pl.BlockSpec((tm, tn), lambda i, j: (i, j)),
                  pl.BlockSpec((None, 1, tn),
                               lambda i, j: ((layer * nb + i // tps) * 6 + which_gt, 0, j))],
        out_specs=pl.BlockSpec((tm, tn), lambda i, j: (i, j)),
        out_shape=jax.ShapeDtypeStruct((t, d), F32),
        compiler_params=_cparams("arbitrary", "arbitrary"),
        name="resid_gemm",
    )(a, w, x, modr)


def _merge_kernel(oa_ref, ob_ref, oc_ref, wa_ref, wb_ref, wc_ref, ga_ref, gb_ref, gc_ref, o_ref):
    def branch(o_r, w_r, g_r):
        y = jnp.dot(o_r[...], w_r[...], preferred_element_type=F32)
        return _sigmoid(g_r[...].astype(F32)) * y

    acc = branch(oa_ref, wa_ref, ga_ref) + branch(ob_ref, wb_ref, gb_ref) + branch(oc_ref, wc_ref, gc_ref)
    o_ref[...] = acc.astype(o_ref.dtype)


def merge_gemm(o_a, o_b, o_c, w_branch, p1, layer, tm, tn):
    t, kb = o_a.shape
    d = w_branch.shape[-1]
    tm = min(tm, t)
    tn = min(tn, d)
    gblk = GATE_OFF // tn
    dblk = d // tn

    def ospec():
        return pl.BlockSpec((tm, kb), lambda i, j: (i, 0))

    def wspec(r):
        return pl.BlockSpec((None, kb, tn), lambda i, j: (layer, r, j))

    def gspec(r):
        return pl.BlockSpec((tm, tn), lambda i, j: (i, gblk + r * dblk + j))

    return pl.pallas_call(
        _merge_kernel,
        grid=(t // tm, d // tn),
        in_specs=[ospec(), ospec(), ospec(), wspec(0), wspec(1), wspec(2), gspec(0), gspec(1), gspec(2)],
        out_specs=pl.BlockSpec((tm, tn), lambda i, j: (i, j)),
        out_shape=jax.ShapeDtypeStruct((t, d), BF16),
        compiler_params=_cparams("arbitrary", "arbitrary"),
        name="merge_gemm",
    )(o_a, o_b, o_c, w_branch, w_branch, w_branch, p1, p1, p1)


def _final_norm_kernel(x_ref, w_ref, o_ref):
    x = x_ref[...]
    ms = jnp.mean(x * x, axis=-1, keepdims=True)
    o_ref[...] = x * lax.rsqrt(ms + NORM_EPS) * w_ref[...]


def final_norm(x, w, tm):
    t, d = x.shape
    tm = min(tm, t)
    return pl.pallas_call(
        _final_norm_kernel,
        grid=(t // tm,),
        in_specs=[pl.BlockSpec((tm, d), lambda i: (i, 0)), pl.BlockSpec((1, d), lambda i: (0, 0))],
        out_specs=pl.BlockSpec((tm, d), lambda i: (i, 0)),
        out_shape=jax.ShapeDtypeStruct((t, d), F32),
        compiler_params=_cparams("arbitrary"),
        name="final_norm",
    )(x, w[None, :])


def _rms_rows(x, w):
    ms = jnp.mean(x * x, axis=-1, keepdims=True)
    return x * lax.rsqrt(ms + NORM_EPS) * w


def _mla_proj_kernel(cq_ref, kr_ref, ckv_ref, cs_ref, qnw_ref, kvnw_ref, wq_ref, wkv_ref,
                     q_ref, k_ref, vt_ref):
    scale =(MLA_NOPE + MLA_ROPE) ** -0.5 * math.log2(math.e)
    cc = cs_ref[:, 0:LANES]
    ss = cs_ref[:, LANES:2 * LANES]
    nq = _rms_rows(cq_ref[...].astype(F32), qnw_ref[...]).astype(BF16)
    q3 = jnp.dot(nq, wq_ref[...], preferred_element_type=F32)
    for h in range(MLA_HEADS):
        b0 = h * 3 * LANES
        nope = q3[:, b0:b0 + LANES]
        rope = q3[:, b0 + LANES:b0 + 2 * LANES] * cc + q3[:, b0 + 2 * LANES:b0 + 3 * LANES] * ss
        q_ref[:, 2 * h * LANES:(2 * h + 1) * LANES] = (nope * scale).astype(BF16)
        q_ref[:, (2 * h + 1) * LANES:(2 * h + 2) * LANES] = (rope * scale).astype(BF16)
    nkv = _rms_rows(ckv_ref[...].astype(F32), kvnw_ref[...]).astype(BF16)
    kv = jnp.dot(nkv, wkv_ref[...], preferred_element_type=F32)
    kr = kr_ref[...].astype(F32)
    krope = (kr[:, 0:LANES] * cc + kr[:, LANES:2 * LANES] * ss).astype(BF16)
    for h in range(MLA_HEADS):
        k_ref[:, 2 * h * LANES:(2 * h + 1) * LANES] = kv[:, h * LANES:(h + 1) * LANES].astype(BF16)
        k_ref[:, (2 * h + 1) * LANES:(2 * h + 2) * LANES] = krope
    ones = jnp.ones((VT_PAD, kv.shape[0]), BF16)
    for h in range(MLA_HEADS):
        vh = kv[:, (MLA_HEADS + h) * LANES:(MLA_HEADS + h + 1) * LANES]
        vt_ref[h, 0:MLA_V, :] = vh.T.astype(BF16)
        vt_ref[h, MLA_V:MLA_V + VT_PAD, :] = ones


def mla_proj(p1, cs, q_norm_w, kv_norm_w, wq3, wkv, layer, nb, seq, tm):
    t = p1.shape[0]
    tm = min(tm, seq)
    tps = seq // tm
    hq = MLA_HEADS * 2 * LANES
    return pl.pallas_call(
        _mla_proj_kernel,
        grid=(t // tm,),
        in_specs=[pl.BlockSpec((tm, MLA_Q_RANK), lambda i: (i, CQ_OFF // MLA_Q_RANK)),
                  pl.BlockSpec((tm, 2 * LANES), lambda i: (i, KR_OFF // (2 * LANES))),
                  pl.BlockSpec((tm, MLA_KV_RANK), lambda i: (i, CKV_OFF // MLA_KV_RANK)),
                  pl.BlockSpec((tm, 2 * LANES), lambda i: (i, 0)),
                  pl.BlockSpec((None, 1, MLA_Q_RANK), lambda i: (layer, 0, 0)),
                  pl.BlockSpec((None, 1, MLA_KV_RANK), lambda i: (layer, 0, 0)),
                  pl.BlockSpec((None, MLA_Q_RANK, 3 * MLA_HEADS * LANES), lambda i: (layer, 0, 0)),
                  pl.BlockSpec((None, MLA_KV_RANK, 2 * MLA_HEADS * LANES), lambda i: (layer, 0, 0))],
        out_specs=[pl.BlockSpec((tm, hq), lambda i: (i, 0)),
                   pl.BlockSpec((tm, hq), lambda i: (i, 0)),
                   pl.BlockSpec((None, MLA_HEADS, MLA_V + VT_PAD, tm), lambda i: (i // tps, 0, 0, i % tps))],
        out_shape=[jax.ShapeDtypeStruct((t, hq), BF16),
                   jax.ShapeDtypeStruct((t, hq), BF16),
                   jax.ShapeDtypeStruct((nb, MLA_HEADS, MLA_V + VT_PAD, seq), BF16)],
        compiler_params=_cparams("arbitrary"),
        name="mla_proj",
    )(p1, p1, p1, cs, q_norm_w, kv_norm_w, wq3, wkv)


def _flash_kernel(it_ref, jt_ref, q_ref, k_ref, vt_ref, o_ref, m_scr, acc_scr):
    t = pl.program_id(2)
    i = it_ref[t]
    j = jt_ref[t]
    heads = range(FLASH_HEADS)

    @pl.when(j == 0)
    def _():
        m_scr[...] = jnp.full(m_scr.shape, -jnp.inf, F32)
        acc_scr[...] = jnp.zeros(acc_scr.shape, F32)

    def step(masked):
        sts = [lax.dot_general(k_ref[:, 2 * g * LANES:2 * (g + 1) * LANES],
                               q_ref[:, 2 * g * LANES:2 * (g + 1) * LANES],
                               (((1,), (1,)), ((), ())), preferred_element_type=F32) for g in heads]
        if masked:
            keep = _iota(sts[0].shape, 1) >= _iota(sts[0].shape, 0)
            sts = [jnp.where(keep, s, -jnp.inf) for s in sts]
        m_prev = [m_scr[g] for g in heads]
        m_new = [jnp.maximum(mp, jnp.max(s, axis=0, keepdims=True)) for mp, s in zip(m_prev, sts)]
        pts = [jnp.exp2(s - mn).astype(BF16) for s, mn in zip(sts, m_new)]
        alphas = [jnp.exp2(mp - mn) for mp, mn in zip(m_prev, m_new)]
        pvs = [jnp.dot(vt_ref[g], pt, preferred_element_type=F32) for g, pt in zip(heads, pts)]
        for g in heads:
            acc_scr[g] = alphas[g] * acc_scr[g] + pvs[g]
            m_scr[g] = m_new[g]

    @pl.when(j < i)
    def _():
        step(False)

    @pl.when(j == i)
    def _():
        step(True)
        for g in heads:
            acc = acc_scr[g]
            o_t = acc[0:MLA_V, :] / acc[MLA_V:MLA_V + 1, :]
            o_ref[:, g * LANES:(g + 1) * LANES] = o_t.T.astype(o_ref.dtype)


def flash_attn(q, k, vt, nb, seq, tq):
    tq = min(tq, seq)
    nq = seq // tq
    hg = FLASH_HEADS
    pairs = [(i, j) for i in range(nq) for j in range(i + 1)]
    it = jnp.asarray([p[0] for p in pairs], jnp.int32)
    jt = jnp.asarray([p[1] for p in pairs], jnp.int32)
    grid_spec = pltpu.PrefetchScalarGridSpec(
        num_scalar_prefetch=2,
        grid=(nb, MLA_HEADS // hg, len(pairs)),
        in_specs=[pl.BlockSpec((tq, hg * 2 * LANES), lambda b, h, t, it_r, jt_r: (b * nq + it_r[t], h)),
                  pl.BlockSpec((tq, hg * 2 * LANES), lambda b, h, t, it_r, jt_r: (b * nq + jt_r[t], h)),
                  pl.BlockSpec((None, hg, MLA_V + VT_PAD, tq), lambda b, h, t, it_r, jt_r: (b, h, 0, jt_r[t]))],
        out_specs=pl.BlockSpec((tq, hg * LANES), lambda b, h, t, it_r, jt_r: (b * nq + it_r[t], h)),
        scratch_shapes=[pltpu.VMEM((hg, 1, tq), F32), pltpu.VMEM((hg, MLA_V + VT_PAD, tq), F32)],
    )
    return pl.pallas_call(
        _flash_kernel,
        grid_spec=grid_spec,
        out_shape=jax.ShapeDtypeStruct((nb * seq, MLA_HEADS * LANES), BF16),
        compiler_params=_cparams("arbitrary", "arbitrary", "arbitrary"),
        name="flash_attn",
    )(it, jt, q, k, vt)


def _shifted_rows(x, halo, d):
    xr = pltpu.roll(x, d, 0)
    hr = pltpu.roll(halo, d, 0)
    top = jnp.where(_iota(halo.shape, 0) < d, hr, xr[0:SUBLANES])
    return jnp.concatenate([top, xr[SUBLANES:]], axis=0)


def _gdn_prep_kernel(q_ref, k_ref, v_ref, qh_ref, kh_ref, vh_ref, cwq_ref, cwk_ref, cwv_ref,
                     p2_ref, alog_ref, dtb_ref, qo_ref, ko_ref, vo_ref, bg_ref, *, tiles_per_seq):
    first = (pl.program_id(0) % tiles_per_seq) == 0

    def conv_silu(x_ref, h_ref, w_ref):
        x = x_ref[...].astype(F32)
        halo = jnp.where(first, 0.0, h_ref[...].astype(F32))
        w = w_ref[...]
        acc = x * w[GDN_CONV - 1:GDN_CONV, :]
        for d in range(1, GDN_CONV):
            acc = acc + _shifted_rows(x, halo, d) * w[GDN_CONV - 1 - d:GDN_CONV - d, :]
        return acc * _sigmoid(acc)

    def l2n(x, mult):
        outs = []
        for h in range(GDN_HEADS):
            xh = x[:, h * LANES:(h + 1) * LANES]
            ss = jnp.sum(xh * xh, axis=-1, keepdims=True)
            outs.append(xh * (lax.rsqrt(ss + 1e-6) * mult))
        return jnp.concatenate(outs, axis=1)

    qo_ref[...] = l2n(conv_silu(q_ref, qh_ref, cwq_ref), GDN_DK ** -0.5).astype(BF16)
    ko_ref[...] = l2n(conv_silu(k_ref, kh_ref, cwk_ref), 1.0).astype(BF16)
    vo_ref[...] = conv_silu(v_ref, vh_ref, cwv_ref).astype(BF16)
    p2 = p2_ref[...]
    lane = _iota(p2.shape, 1)
    g = -jnp.exp(alog_ref[...]) * _softplus(p2 + dtb_ref[...])
    bg_ref[...] = jnp.where(lane < GDN_HEADS, _sigmoid(p2), g)


def gdn_prep(p1, p2, conv_w, a_log, dt_bias, layer, seq, tm):
    t = p1.shape[0]
    tm = min(tm, seq)
    w = GDN_HEADS * GDN_DK
    cblk = GQ_OFF // w
    hb = tm // SUBLANES

    def xspec(r):
        return pl.BlockSpec((tm, w), lambda i: (i, cblk + r))

    def hspec(r):
        return pl.BlockSpec((SUBLANES, w), lambda i: (jnp.maximum(i * hb - 1, 0), cblk + r))

    def cwspec(r):
        return pl.BlockSpec((None, GDN_CONV, w), lambda i: (layer, 0, r))

    pad = jnp.zeros((LANES - 2 * GDN_HEADS,), F32)
    z8 = jnp.zeros((GDN_HEADS,), F32)
    alog_row = jnp.concatenate([z8, a_log[layer], pad])[None, :]
    dtb_row = jnp.concatenate([z8, dt_bias[layer], pad])[None, :]
    return pl.pallas_call(
        functools.partial(_gdn_prep_kernel, tiles_per_seq=seq // tm),
        grid=(t // tm,),
        in_specs=[xspec(0), xspec(1), xspec(2), hspec(0), hspec(1), hspec(2),
                  cwspec(0), cwspec(1), cwspec(2),
                  pl.BlockSpec((tm, LANES), lambda i: (i, 2)),
                  pl.BlockSpec((1, LANES), lambda i: (0, 0)),
                  pl.BlockSpec((1, LANES), lambda i: (0, 0))],
        out_specs=[pl.BlockSpec((tm, w), lambda i: (i, 0))] * 3 + [pl.BlockSpec((tm, LANES), lambda i: (i, 0))],
        out_shape=[jax.ShapeDtypeStruct((t, w), BF16)] * 3 + [jax.ShapeDtypeStruct((t, LANES), F32)],
        compiler_params=_cparams("arbitrary"),
        name="gdn_prep",
    )(p1, p1, p1, p1, p1, p1, conv_w, conv_w, conv_w, p2, alog_row, dtb_row)


def _gdn_chunk_kernel(q_ref, k_ref, v_ref, z_ref, bg_ref, nw_ref, o_ref, st_ref, uw_scr, qk_scr, qg_scr, kg_scr):
    nb, tb, _ = q_ref.shape
    c = CHUNK
    nck = tb // c

    @pl.when(pl.program_id(0) == 0)
    def _():
        st_ref[...] = jnp.zeros(st_ref.shape, F32)

    r = _iota((tb, tb), 0)
    cc = _iota((tb, tb), 1)
    same = (r // c) == (cc // c)
    incl = same & (r >= cc)
    strict = same & (r > cc)
    units = [(b, h) for b in range(nb) for h in range(GDN_HEADS)]
    lmat = jnp.where(incl, 1.0, 0.0).astype(BF16)
    g_cum = [_dot_exact_lhs(lmat, bg_ref[b], 3) for b in range(nb)]
    g_tot = [jnp.concatenate([jnp.broadcast_to(g[(ci + 1) * c - 1:(ci + 1) * c, :], (c, LANES))
                              for ci in range(nck)], axis=0) for g in g_cum]
    g_cum_t = [g.T for g in g_cum]

    def cols(h):
        return slice(h * LANES, (h + 1) * LANES)

    def glane(h):
        return slice(GDN_HEADS + h, GDN_HEADS + h + 1)

    for g0 in range(0, len(units), GROUP):
        grp = units[g0:g0 + GROUP]
        ks = [k_ref[b, :, cols(h)] for b, h in grp]
        gcs = [g_cum[b][:, glane(h)] for b, h in grp]
        betas = [bg_ref[b, :, h:h + 1] for b, h in grp]
        decs = [jnp.exp(jnp.where(incl, gc - g_cum_t[b][glane(h), :], -jnp.inf))
                for gc, (b, h) in zip(gcs, grp)]
        qs = [q_ref[b, :, cols(h)] for b, h in grp]
        kqs = [_dot_nt(jnp.concatenate([k, q], axis=0), k) for k, q in zip(ks, qs)]
        a_s = [jnp.where(strict, kq[:tb] * dec, 0.0) * beta for kq, dec, beta in zip(kqs, decs, betas)]
        ns = _tri_inv_m1(a_s, c)
        for idx, (b, h) in enumerate(grp):
            u = g0 + idx
            q = qs[idx]
            kf = ks[idx].astype(F32)
            vf = v_ref[b, :, cols(h)].astype(F32)
            eg = jnp.exp(gcs[idx])
            rhs = jnp.concatenate([betas[idx] * vf, (betas[idx] * eg) * kf], axis=1)
            uw_scr[u] = rhs + _dot(ns[idx], rhs)
            qk_scr[u] = (kqs[idx][tb:] * decs[idx]).astype(BF16)
            qg_scr[u] = (q.astype(F32) * eg).astype(BF16)
            kg_scr[u] = (kf * jnp.exp(g_tot[b][:, glane(h)] - gcs[idx])).astype(BF16)

    states = [st_ref[u] for u in range(len(units))]
    w_parts = [[] for _ in units]
    og_parts = [[] for _ in units]
    for ci in range(nck):
        rows = slice(ci * c, (ci + 1) * c)
        sbs = [s.astype(BF16) for s in states]
        wqs = [_dot(jnp.concatenate([uw_scr[u, rows, GDN_DV:].astype(BF16), qg_scr[u, rows, :]], axis=0), sbs[u])
               for u in range(len(units))]
        for u in range(len(units)):
            w_parts[u].append((uw_scr[u, rows, :GDN_DV] - wqs[u][:c]).astype(BF16))
            og_parts[u].append(wqs[u][c:])
        states = [jnp.exp(g_tot[b][ci * c:ci * c + 1, glane(h)]) * states[u]
                  + _dot_tn(kg_scr[u, rows, :], w_parts[u][ci])
                  for u, (b, h) in enumerate(units)]

    nw = nw_ref[...]
    for u, (b, h) in enumerate(units):
        o = jnp.concatenate(og_parts[u], axis=0) + _dot(qk_scr[u], jnp.concatenate(w_parts[u], axis=0))
        zf = z_ref[b, :, cols(h)].astype(F32)
        o_ref[b, :, cols(h)] = (_rms_rows(o, nw) * (zf * _sigmoid(zf))).astype(o_ref.dtype)
        st_ref[u] = states[u]


def gdn_chunk(qn, kn, vc, p1, bg, norm_w, layer, nb, seq):
    t, w = qn.shape
    tb = min(GDN_STEP, seq)
    zblk = (GQ_OFF + 3 * w) // w
    nu = nb * GDN_HEADS

    def xspec(cb):
        return pl.BlockSpec((nb, tb, w), lambda s: (0, s, cb))

    out = pl.pallas_call(
        _gdn_chunk_kernel,
        grid=(seq // tb,),
        in_specs=[xspec(0), xspec(0), xspec(0), xspec(zblk),
                  pl.BlockSpec((nb, tb, LANES), lambda s: (0, s, 0)),
                  pl.BlockSpec((None, 1, GDN_DV), lambda s: (layer, 0, 0))],
        out_specs=xspec(0),
        out_shape=jax.ShapeDtypeStruct((nb, seq, w), BF16),
        scratch_shapes=[pltpu.VMEM((nu, GDN_DK, GDN_DV), F32),
                        pltpu.VMEM((nu, tb, 2 * GDN_DV), F32),
                        pltpu.VMEM((nu, tb, tb), BF16),
                        pltpu.VMEM((nu, tb, GDN_DK), BF16),
                        pltpu.VMEM((nu, tb, GDN_DK), BF16)],
        compiler_params=_cparams("arbitrary"),
        name="gdn_chunk",
    )(qn.reshape(nb, seq, w), kn.reshape(nb, seq, w), vc.reshape(nb, seq, w),
      p1.reshape(nb, seq, p1.shape[-1]), bg.reshape(nb, seq, LANES), norm_w)
    return out.reshape(t, w)


def _head_sum_matrix():
    return ((_iota((LANES, LANES), 0) // RWKV_HEAD) == (_iota((LANES, LANES), 1) // RWKV_HEAD)).astype(BF16)


def _rwkv_prep_kernel(r_ref, k_ref, v_ref, x_ref, rh_ref, kh_ref, vh_ref, xh_ref,
                      mur_ref, muk_ref, muv_ref, mux_ref, w0_ref, wup_ref, a0_ref, aup_ref, gup_ref,
                      kk_w_ref, ka_ref, rk_ref,
                      kkd_ref, rt_ref, kt_ref, bt_ref, kh_o_ref, bh_o_ref, v_o_ref, pc_ref, bonus_ref, g_ref,
                      *, tiles_per_seq):
    first = (pl.program_id(0) % tiles_per_seq) == 0
    c = CHUNK
    tm = r_ref.shape[0]

    def shift_mix(x_r, h_r, mu_r):
        x = x_r[...].astype(F32)
        halo = jnp.where(first, 0.0, h_r[...].astype(F32))
        prev = _shifted_rows(x, halo, 1)
        return x + (prev - x) * mu_r[...]

    r = shift_mix(r_ref, rh_ref, mur_ref)
    k = shift_mix(k_ref, kh_ref, muk_ref)
    v = shift_mix(v_ref, vh_ref, muv_ref)
    xs = shift_mix(x_ref, xh_ref, mux_ref)
    xw = xs[:, 0:LANES]
    lane = _iota(xw.shape, 1)
    tw = jnp.where(lane < RWKV_W_RANK, jnp.tanh(xw), 0.0)
    xa = jnp.where(lane >= RWKV_W_RANK, xw, 0.0)
    sg = _sigmoid(xs[:, LANES:2 * LANES])
    w_log = -_softplus(-(w0_ref[...] + _dot(tw, wup_ref[...]))) - 0.5
    lw = -jnp.exp(w_log)
    a = _sigmoid(a0_ref[...] + _dot(xa, aup_ref[...]))
    g = _dot(sg, gup_ref[...])
    hs = _head_sum_matrix()
    kkw = k * kk_w_ref[...]
    k2 = k * (1.0 + (a - 1.0) * ka_ref[...])
    rkr = r * k2 * rk_ref[...]
    kk_parts = []
    bonus_parts = []
    for blk in range(RWKV_WIDTH // LANES):
        cols = slice(blk * LANES, (blk + 1) * LANES)
        x_blk = kkw[:, cols]
        ss = _dot_exact_rhs(x_blk * x_blk, hs, 2)
        kk_parts.append(x_blk * lax.rsqrt(ss + 1e-6))
        bonus_parts.append(_dot_exact_rhs(rkr[:, cols], hs, 2) * v[:, cols])
    kk = jnp.concatenate(kk_parts, axis=1)
    bonus_ref[...] = jnp.concatenate(bonus_parts, axis=1).astype(bonus_ref.dtype)
    g_ref[...] = g.astype(g_ref.dtype)
    b = kk * a
    ri = _iota((tm, tm), 0)
    cj = _iota((tm, tm), 1)
    same = (ri // c) == (cj // c)
    gi = _dot_exact_lhs(jnp.where(same & (ri >= cj), 1.0, 0.0).astype(BF16), lw, 3)
    ge = gi - lw
    e_neg = jnp.exp(-gi)
    kkd_ref[...] = (kk * jnp.exp(ge)).astype(BF16)
    rt_ref[...] = (r * jnp.exp(gi)).astype(BF16)
    kt_ref[...] = (k2 * e_neg).astype(BF16)
    bt_ref[...] = (b * e_neg).astype(BF16)
    v_o_ref[...] = v.astype(BF16)
    for ci in range(tm // c):
        rows = slice(ci * c, (ci + 1) * c)
        egl = jnp.exp(gi[(ci + 1) * c - 1:(ci + 1) * c, :])
        pc_ref[ci] = egl
        e_tail = egl * e_neg[rows]
        kh_o_ref[rows, :] = (k2[rows] * e_tail).astype(BF16)
        bh_o_ref[rows, :] = (b[rows] * e_tail).astype(BF16)


def rwkv_prep(p1, p2, mu, w0, w_up, a0, a_up, g_up, k_k, k_a, r_k, layer, seq, tm):
    t = p1.shape[0]
    tm = min(tm, seq)
    w = RWKV_WIDTH
    cblk = RR_OFF // w
    hb = tm // SUBLANES
    xw2 = 2 * LANES

    def xspec(r):
        return pl.BlockSpec((tm, w), lambda i: (i, cblk + r))

    def hspec(r):
        return pl.BlockSpec((SUBLANES, w), lambda i: (jnp.maximum(i * hb - 1, 0), cblk + r))

    def row(width, blk):
        return pl.BlockSpec((None, 1, width), lambda i: (layer, 0, blk))

    zpad = jnp.zeros((w_up.shape[0], RWKV_W_RANK, w), F32)
    wup_p = jnp.concatenate([w_up, zpad], axis=1).astype(BF16)
    aup_p = jnp.concatenate([zpad, a_up], axis=1).astype(BF16)
    gup_b = g_up.astype(BF16)
    mu3 = mu[:, None, :]
    rk_row = r_k.reshape(r_k.shape[0], 1, w)
    outs = [jax.ShapeDtypeStruct((t, w), BF16)] * 7
    outs += [jax.ShapeDtypeStruct((t // CHUNK, 1, w), F32),
             jax.ShapeDtypeStruct((t, w), BF16), jax.ShapeDtypeStruct((t, w), BF16)]
    ospec = pl.BlockSpec((tm, w), lambda i: (i, 0))
    out_specs = [ospec] * 7 + [pl.BlockSpec((tm // CHUNK, 1, w), lambda i: (i, 0, 0)), ospec, ospec]
    return pl.pallas_call(
        functools.partial(_rwkv_prep_kernel, tiles_per_seq=seq // tm),
        grid=(t // tm,),
        in_specs=[xspec(0), xspec(1), xspec(2), pl.BlockSpec((tm, xw2), lambda i: (i, 0)),
                  hspec(0), hspec(1), hspec(2),
                  pl.BlockSpec((SUBLANES, xw2), lambda i: (jnp.maximum(i * hb - 1, 0), 0)),
                  row(w, 0), row(w, 1), row(w, 2),
                  pl.BlockSpec((None, 1, xw2), lambda i: (layer, 0, 3 * w // xw2)),
                  row(w, 0),
                  pl.BlockSpec((None, LANES, w), lambda i: (layer, 0, 0)),
                  row(w, 0),
                  pl.BlockSpec((None, LANES, w), lambda i: (layer, 0, 0)),
                  pl.BlockSpec((None, RWKV_G_RANK, w), lambda i: (layer, 0, 0)),
                  row(w, 0), row(w, 0), row(w, 0)],
        out_specs=out_specs,
        out_shape=outs,
        compiler_params=_cparams("arbitrary"),
        name="rwkv_prep",
    )(p1, p1, p1, p2, p1, p1, p1, p2, mu3, mu3, mu3, mu3, w0[:, None, :], wup_p, a0[:, None, :], aup_p,
      gup_b, k_k[:, None, :], k_a[:, None, :], rk_row)


def _rwkv_chunk_kernel(kkd_ref, rt_ref, kt_ref, bt_ref, kh_ref, bh_ref, v_ref, pc_ref, y_ref, st_ref,
                       n_scr, arb_scr, uloc_scr, yloc_scr):
    nb, tb, w = kkd_ref.shape
    c = CHUNK
    nck = tb // c
    n2 = 2 * tb
    npair = w // LANES

    @pl.when(pl.program_id(0) == 0)
    def _():
        st_ref[...] = jnp.zeros(st_ref.shape, F32)

    r = _iota((n2, n2), 0)
    cc = _iota((n2, n2), 1)
    same = (r // c) == (cc // c)
    incl = same & (r >= cc)
    strict = same & (r > cc)
    head0 = _iota((1, LANES), 1) < RWKV_HEAD
    bd = (_iota((LANES, LANES), 0) // RWKV_HEAD) == (_iota((LANES, LANES), 1) // RWKV_HEAD)
    units = [(b, p) for b in range(nb) for p in range(npair)]

    def cols(p):
        return slice(p * LANES, (p + 1) * LANES)

    def rows(ci):
        return slice(ci * c, (ci + 1) * c)

    def blk(ci):
        return slice(ci * LANES, (ci + 1) * LANES)

    def stack_heads(x):
        x0 = jnp.where(head0, x, jnp.zeros_like(x))
        x1 = jnp.where(head0, jnp.zeros_like(x), x)
        return jnp.concatenate([piece[rows(ci)] for ci in range(nck) for piece in (x0, x1)], axis=0)

    def stack_dup(x):
        return jnp.concatenate([x[rows(ci)] for ci in range(nck) for _ in range(2)], axis=0)

    for g0 in range(0, len(units), GROUP):
        grp = units[g0:g0 + GROUP]
        lks = [stack_heads(kkd_ref[b, :, cols(p)]) for b, p in grp]
        lrs = [stack_heads(rt_ref[b, :, cols(p)]) for b, p in grp]
        rks = [stack_dup(kt_ref[b, :, cols(p)]) for b, p in grp]
        rbs = [stack_dup(bt_ref[b, :, cols(p)]) for b, p in grp]
        v2s = [stack_dup(v_ref[b, :, cols(p)]) for b, p in grp]
        pws = [[_dot_nt(jnp.concatenate([lk[blk(ci)], lr[blk(ci)]], axis=0),
                        jnp.concatenate([rk[blk(ci)], rb[blk(ci)]], axis=0)) for ci in range(nck)]
               for lk, lr, rk, rb in zip(lks, lrs, rks, rbs)]
        zblk = jnp.zeros((LANES, LANES), F32)

        def chunk_diag(pw, rsl, csl):
            return jnp.concatenate(
                [jnp.concatenate([pw[ci][rsl, csl] if cj == ci else zblk for cj in range(nck)], axis=1)
                 for ci in range(nck)], axis=0)

        lo, hi = slice(0, LANES), slice(LANES, 2 * LANES)
        a_kb = [jnp.where(strict, chunk_diag(pw, lo, hi), 0.0) for pw in pws]
        a_kk = [jnp.where(strict, chunk_diag(pw, lo, lo), 0.0) for pw in pws]
        ns = _tri_inv_m1(a_kb, c)
        avs = [_dot(a, v2) for a, v2 in zip(a_kk, v2s)]
        for idx in range(len(grp)):
            u = g0 + idx
            nbf = ns[idx].astype(BF16)
            n_scr[u] = nbf
            uloc_scr[u] = avs[idx] + _dot(nbf, avs[idx])
            a_rk = jnp.where(incl, chunk_diag(pws[idx], hi, lo), 0.0)
            yloc_scr[u] = _dot(a_rk, v2s[idx])
            arb_scr[u] = jnp.where(incl, chunk_diag(pws[idx], hi, hi), 0.0).astype(BF16)

    nu = len(units)
    states = [st_ref[u] for u in range(nu)]
    rs_parts = [[] for _ in units]
    ub_parts = [[] for _ in units]
    for ci in range(nck):
        sbs = [s.astype(BF16) for s in states]
        kss = [_dot_nt(jnp.concatenate([kkd_ref[b, rows(ci), cols(p)], rt_ref[b, rows(ci), cols(p)]], axis=0), sbs[u])
               for u, (b, p) in enumerate(units)]
        ks2 = [jnp.concatenate([ks[:c], ks[:c]], axis=0) for ks in kss]
        tks = [ks2[u] + _dot(n_scr[u, blk(ci), blk(ci)], ks2[u]) for u in range(nu)]
        for u in range(nu):
            u2 = uloc_scr[u, blk(ci), :] + tks[u]
            ub_parts[u].append(jnp.where(head0, u2[:c], u2[c:]).astype(BF16))
            rs_parts[u].append(kss[u][c:])
        states = [states[u] * pc_ref[b, ci, :, cols(p)]
                  + jnp.where(bd, _dot_tn(jnp.concatenate([v_ref[b, rows(ci), cols(p)], ub_parts[u][ci]], axis=0),
                                          jnp.concatenate([kh_ref[b, rows(ci), cols(p)],
                                                           -bh_ref[b, rows(ci), cols(p)]], axis=0)), 0.0)
                  for u, (b, p) in enumerate(units)]

    for u, (b, p) in enumerate(units):
        u2all = jnp.concatenate([ub_parts[u][ci] for ci in range(nck) for _ in range(2)], axis=0)
        yall = yloc_scr[u] - _dot(arb_scr[u], u2all)
        for ci in range(nck):
            yb = yall[blk(ci), :]
            y_ref[b, rows(ci), cols(p)] = rs_parts[u][ci] + jnp.where(head0, yb[:c], yb[c:])
        st_ref[u] = states[u]


def rwkv_chunk(tensors, pc, nb, seq):
    t, w = tensors[0].shape
    tb = min(RWKV_STEP, seq)
    nck = tb // CHUNK
    nu = nb * (w // LANES)
    xspec = pl.BlockSpec((nb, tb, w), lambda s: (0, s, 0))
    out = pl.pallas_call(
        _rwkv_chunk_kernel,
        grid=(seq // tb,),
        in_specs=[xspec] * 7 + [pl.BlockSpec((nb, nck, 1, w), lambda s: (0, s, 0, 0))],
        out_specs=xspec,
        out_shape=jax.ShapeDtypeStruct((nb, seq, w), F32),
        scratch_shapes=[pltpu.VMEM((nu, LANES, LANES), F32),
                        pltpu.VMEM((nu, 2 * tb, 2 * tb), BF16),
                        pltpu.VMEM((nu, 2 * tb, 2 * tb), BF16),
                        pltpu.VMEM((nu, 2 * tb, LANES), F32),
                        pltpu.VMEM((nu, 2 * tb, LANES), F32)],
        compiler_params=_cparams("arbitrary"),
        name="rwkv_chunk",
    )(*[x.reshape(nb, seq, w) for x in tensors], pc.reshape(nb, seq // CHUNK, 1, w))
    return out.reshape(t, w)


def _rwkv_post_kernel(y_ref, bonus_ref, g_ref, lw_ref, lb_ref, o_ref):
    hm = _head_sum_matrix() * (1.0 / RWKV_HEAD)
    hm = hm.astype(BF16)
    for blk in range(RWKV_WIDTH // LANES):
        cols = slice(blk * LANES, (blk + 1) * LANES)
        y = y_ref[:, cols]
        mean = _dot_exact_rhs(y, hm, 3)
        yc = y - mean
        var = _dot_exact_rhs(yc * yc, hm, 3)
        yn = yc * lax.rsqrt(var + RWKV_LN_EPS) * lw_ref[:, cols] + lb_ref[:, cols]
        out = (yn + bonus_ref[:, cols].astype(F32)) * g_ref[:, cols].astype(F32)
        o_ref[:, cols] = out.astype(o_ref.dtype)


def rwkv_post(y, bonus, g, lnx_w, lnx_b, layer, tm):
    t, w = y.shape
    tm = min(tm, t)
    xspec = pl.BlockSpec((tm, w), lambda i: (i, 0))
    rspec = pl.BlockSpec((None, 1, w), lambda i: (layer, 0, 0))
    return pl.pallas_call(
        _rwkv_post_kernel,
        grid=(t // tm,),
        in_specs=[xspec, xspec, xspec, rspec, rspec],
        out_specs=xspec,
        out_shape=jax.ShapeDtypeStruct((t, w), BF16),
        compiler_params=_cparams("arbitrary"),
        name="rwkv_post",
    )(y, bonus, g, lnx_w[:, None, :], lnx_b[:, None, :])


def _relayout_w_in(w_in):
    nl, d, _ = w_in.shape
    wt = jnp.swapaxes(w_in, 1, 2).astype(BF16)

    def rows(lo, n):
        return wt[:, lo:lo + n, :]

    def zeros(n):
        return jnp.zeros((nl, n, d), BF16)

    kr = rows(_O_KR, MLA_ROPE)
    half = MLA_ROPE // 2
    kr_swap = jnp.concatenate([-kr[:, half:, :], kr[:, :half, :]], axis=1)
    w1 = jnp.concatenate([
        rows(_O_CQ, MLA_Q_RANK),
        kr, zeros(LANES - MLA_ROPE), kr_swap, zeros(LANES - MLA_ROPE),
        rows(_O_CKV, MLA_KV_RANK), zeros(GQ_OFF - CKV_OFF - MLA_KV_RANK),
        rows(_O_GDN, 4096),
        rows(_O_RWKV, 3072),
        rows(_O_GATE, 3 * D_MODEL)], axis=1)
    w2 = jnp.concatenate([rows(_O_XW, 256), rows(_O_GB, 16), zeros(NP2 - 272)], axis=1)
    return w1, w2


def _relayout_w_uq(w_uq):
    nl, r, _ = w_uq.shape
    wq = w_uq.reshape(nl, r, MLA_HEADS, MLA_NOPE + MLA_ROPE)
    nope = wq[..., :MLA_NOPE]
    rope = wq[..., MLA_NOPE:]
    half = MLA_ROPE // 2
    swap = jnp.concatenate([-rope[..., half:], rope[..., :half]], axis=-1)
    z = jnp.zeros(rope.shape[:-1] + (LANES - MLA_ROPE,), w_uq.dtype)
    return jnp.concatenate([nope, rope, z, swap, z], axis=-1).reshape(nl, r, 3 * MLA_HEADS * LANES).astype(BF16)


def _relayout_w_ukv(w_ukv):
    nl, r, _ = w_ukv.shape
    wkv = w_ukv.reshape(nl, r, MLA_HEADS, MLA_NOPE + MLA_V)
    wk = wkv[..., :MLA_NOPE].reshape(nl, r, MLA_HEADS * MLA_NOPE)
    wv = wkv[..., MLA_NOPE:].reshape(nl, r, MLA_HEADS * MLA_V)
    return jnp.concatenate([wk, wv], axis=-1).astype(BF16)


def _rope_table(positions):
    inv = 1.0 / (ROPE_THETA ** (jnp.arange(0, MLA_ROPE, 2, dtype=F32) / MLA_ROPE))
    ang = positions.astype(F32)[..., None] * inv
    cos, sin = jnp.cos(ang), jnp.sin(ang)
    z = jnp.zeros(cos.shape[:-1] + (LANES - MLA_ROPE,), F32)
    cs = jnp.concatenate([cos, cos, z, sin, sin, z], axis=-1)
    return cs.reshape(-1, 2 * LANES)


def kernel(x, c, positions, w_ada, b_ada, norm1_w, w_in, mla_q_norm_w, mla_w_uq, mla_kv_norm_w, mla_w_ukv,
           gdn_conv_w, gdn_a_log, gdn_dt_bias, gdn_norm_w, rwkv_mu, rwkv_w0, rwkv_w_up, rwkv_a0, rwkv_a_up,
           rwkv_g_up, rwkv_k_k, rwkv_k_a, rwkv_r_k, rwkv_lnx_w, rwkv_lnx_b, w_branch, w_out, norm2_w,
           w_gate_up, w_down, final_norm_w):
    nb, seq, d = x.shape
    nl = w_in.shape[0]
    assert seq % (2 * CHUNK) == 0 and d == D_MODEL

    w1, w2 = _relayout_w_in(w_in)
    wq3 = _relayout_w_uq(mla_w_uq)
    wkv = _relayout_w_ukv(mla_w_ukv)
    w_branch_b = w_branch.astype(BF16)
    w_out_b = w_out.astype(BF16)
    w_gu_b = w_gate_up.astype(BF16)
    w_down_b = w_down.astype(BF16)
    cs = _rope_table(positions)

    mod = adaln_mod(c, w_ada, b_ada)
    modr = mod.reshape(nl * nb * 6, 1, d)
    n1 = norm1_w[:, None, :]
    n2 = norm2_w[:, None, :]
    qnw = mla_q_norm_w[:, None, :]
    kvnw = mla_kv_norm_w[:, None, :]
    gnw = gdn_norm_w[:, None, :]

    xf = x.reshape(nb * seq, d)
    for l in range(nl):
        p1, p2 = norm_gemm(xf, n1, modr, w1, w2, l, 0, nb, seq, *GEMM_TILES["w_in"])
        q, k, vt = mla_proj(p1, cs, qnw, kvnw, wq3, wkv, l, nb, seq, ROW_TILE)
        o_a = flash_attn(q, k, vt, nb, seq, ATTN_TILE)
        qn, kn, vc, bg = gdn_prep(p1, p2, gdn_conv_w, gdn_a_log, gdn_dt_bias, l, seq, ROW_TILE)
        o_b = gdn_chunk(qn, kn, vc, p1, bg, gnw, l, nb, seq)
        prep = rwkv_prep(p1, p2, rwkv_mu, rwkv_w0, rwkv_w_up, rwkv_a0, rwkv_a_up, rwkv_g_up,
                         rwkv_k_k, rwkv_k_a, rwkv_r_k, l, seq, ROW_TILE)
        y = rwkv_chunk(prep[:7], prep[7], nb, seq)
        o_c = rwkv_post(y, prep[8], prep[9], rwkv_lnx_w, rwkv_lnx_b, l, ROW_TILE)
        merged = merge_gemm(o_a, o_b, o_c, w_branch_b, p1, l, *GEMM_TILES["merge"])
        xf = resid_gemm(merged, w_out_b, xf, modr, l, 2, nb, seq, *GEMM_TILES["w_out"])
        act = ffn_up(xf, n2, modr, w_gu_b, l, nb, seq, *GEMM_TILES["ffn_up"])
        xf = resid_gemm(act, w_down_b, xf, modr, l, 5, nb, seq, *GEMM_TILES["ffn_down"])
    return final_norm(xf, final_norm_w, ROW_TILE).reshape(nb, seq, d)
```

```python
import functools
import math

import jax
import jax.numpy as jnp
from jax import lax
from jax.experimental import pallas as pl
from jax.experimental.pallas import tpu as pltpu

F32 = jnp.float32
BF16 = jnp.bfloat16

D_MODEL = 2048
MLA_HEADS = 8
MLA_Q_RANK = 768
MLA_KV_RANK = 512
MLA_NOPE = 128
MLA_ROPE = 64
MLA_V = 128
ROPE_THETA = 10000.0
GDN_HEADS = 8
GDN_DK = 128
GDN_DV = 128
GDN_CONV = 4
RWKV_HEADS = 16
RWKV_HEAD = 64
RWKV_WIDTH = RWKV_HEADS * RWKV_HEAD
RWKV_W_RANK = 64
RWKV_A_RANK = 64
RWKV_G_RANK = 128
RWKV_LN_EPS = 64e-5
D_FF = 5632
NORM_EPS = 1e-6
CHUNK = 64
MXU_TILE = 256
GDN_STEP = MXU_TILE
RWKV_STEP = MXU_TILE // 2
GROUP = 8
FLASH_HEADS = 8
VT_PAD = 16
LANES = 128
SUBLANES = 8
VMEM_LIMIT = 56 * 1024 * 1024

CQ_OFF = 0
KR_OFF = 768
CKV_OFF = 1024
GQ_OFF = 2048
RR_OFF = 6144
GATE_OFF = 9216
NP1 = 15360
NP2 = 384

GEMM_TILES = {
    "w_in": (1024, 1536),
    "merge": (1024, 1024),
    "w_out": (1024, 1024),
    "ffn_up": (1024, 512),
    "ffn_down": (1024, 512),
}
ROW_TILE = 512
ATTN_TILE = 512

_O_CQ = 0
_O_CKV = 768
_O_KR = 1280
_O_GDN = 1344
_O_GZ = _O_GDN + 3072
_O_GB = _O_GZ + 1024
_O_GA = _O_GB + 8
_O_RWKV = _O_GA + 8
_O_XW = _O_RWKV + 3072
_O_XA = _O_XW + 64
_O_XG = _O_XA + 64
_O_GATE = _O_XG + 128


def _cparams(*sem):
    return pltpu.CompilerParams(dimension_semantics=sem, vmem_limit_bytes=VMEM_LIMIT)


def _dot(a, b):
    return jnp.dot(a.astype(BF16), b.astype(BF16), preferred_element_type=F32)


def _dot_nt(a, b):
    return lax.dot_general(a.astype(BF16), b.astype(BF16), (((1,), (1,)), ((), ())),
                           preferred_element_type=F32)


def _dot_tn(a, b):
    return lax.dot_general(a.astype(BF16), b.astype(BF16), (((0,), (0,)), ((), ())),
                           preferred_element_type=F32)


def _split(a, terms):
    parts = []
    rem = a
    for _ in range(terms):
        p = rem.astype(BF16)
        parts.append(p)
        rem = rem - p.astype(F32)
    return parts


def _dot_exact_lhs(a_bf, b, terms):
    out = None
    for p in _split(b, terms):
        t = _dot(a_bf, p)
        out = t if out is None else out + t
    return out


def _dot_exact_rhs(a, b_bf, terms):
    out = None
    for p in _split(a, terms):
        t = _dot(p, b_bf)
        out = t if out is None else out + t
    return out


def _iota(shape, dim):
    return lax.broadcasted_iota(jnp.int32, shape, dim)


def _tri_inv_m1(a_list, n):
    m = a_list[0].shape[0]
    nblk = m // n
    lane_blk = _iota((n, m), 1) // n

    def compact(x):
        out = x[0:n]
        for i in range(1, nblk):
            out = out + x[i * n:(i + 1) * n]
        return out

    def expand(xc):
        zero = jnp.zeros_like(xc)
        return jnp.concatenate([jnp.where(lane_blk == i, xc, zero) for i in range(nblk)], axis=0)

    a_bf = [a.astype(BF16) for a in a_list]
    acs = [compact(a) for a in a_list]
    ns = [-ac for ac in acs]
    ps = [_dot(ac, ab) for ac, ab in zip(acs, a_bf)]
    k = 2
    while True:
        pbds = [expand(p.astype(BF16)) for p in ps]
        both = [_dot(jnp.concatenate([nn, p], axis=0), pbd) for nn, p, pbd in zip(ns, ps, pbds)]
        ns = [nn + p + bo[0:n] for nn, p, bo in zip(ns, ps, both)]
        k *= 2
        if k >= n:
            break
        ps = [bo[n:2 * n] for bo in both]
    return [expand(nn) for nn in ns]


def _sigmoid(x):
    return jax.nn.sigmoid(x)


def _softplus(x):
    return jnp.maximum(x, 0.0) + jnp.log1p(jnp.exp(-jnp.abs(x)))


def _mod_kernel(c_ref, w_ref, b_ref, o_ref):
    w = w_ref[...]
    for m in range(c_ref.shape[0]):
        c = c_ref[m]
        ca = c * _sigmoid(c)
        o_ref[m:m + 1, :] = jnp.sum(ca * w, axis=0, keepdims=True) + b_ref[...]


def adaln_mod(c, w_ada, b_ada):
    nl, d, n = w_ada.shape
    b = c.shape[0]
    tn = min(1024, n)
    return pl.pallas_call(
        _mod_kernel,
        grid=(nl, n // tn),
        in_specs=[pl.BlockSpec((b, d, 1), lambda l, j: (0, 0, 0)),
                  pl.BlockSpec((None, d, tn), lambda l, j: (l, 0, j)),
                  pl.BlockSpec((None, 1, tn), lambda l, j: (l, 0, j))],
        out_specs=pl.BlockSpec((None, b, tn), lambda l, j: (l, 0, j)),
        out_shape=jax.ShapeDtypeStruct((nl, b, n), F32),
        compiler_params=_cparams("arbitrary", "arbitrary"),
        name="adaln_mod",
    )(c[:, :, None], w_ada, b_ada[:, None, :])


def _modulate_to(h_scr, x_ref, nw_ref, sh_ref, sc_ref):
    x = x_ref[...]
    ms = jnp.mean(x * x, axis=-1, keepdims=True)
    y = x * lax.rsqrt(ms + NORM_EPS) * nw_ref[...]
    h_scr[...] = (y * (1.0 + sc_ref[...]) + sh_ref[...]).astype(BF16)


def _norm_gemm_kernel(x_ref, nw_ref, sh_ref, sc_ref, w_ref, w2_ref, o_ref, o2_ref, h_scr):
    @pl.when(pl.program_id(1) == 0)
    def _():
        _modulate_to(h_scr, x_ref, nw_ref, sh_ref, sc_ref)
        o2_ref[...] = _dot_nt(h_scr[...], w2_ref[...])

    o_ref[...] = _dot_nt(h_scr[...], w_ref[...]).astype(o_ref.dtype)


def _mod_spec(d, layer, nb, which, tiles_per_seq):
    return pl.BlockSpec((None, 1, d),
                        lambda i, j: ((layer * nb + i // tiles_per_seq) * 6 + which, 0, 0))


def norm_gemm(x, norm_w, modr, w, w2, layer, which_sh, nb, seq, tm, tn):
    t, d = x.shape
    n = w.shape[1]
    n2 = w2.shape[1]
    tm = min(tm, seq)
    tn = min(tn, n)
    tps = seq // tm
    return pl.pallas_call(
        _norm_gemm_kernel,
        grid=(t // tm, n // tn),
        in_specs=[pl.BlockSpec((tm, d), lambda i, j: (i, 0)),
                  pl.BlockSpec((None, 1, d), lambda i, j: (layer, 0, 0)),
                  _mod_spec(d, layer, nb, which_sh, tps),
                  _mod_spec(d, layer, nb, which_sh + 1, tps),
                  pl.BlockSpec((None, tn, d), lambda i, j: (layer, j, 0)),
                  pl.BlockSpec((None, n2, d), lambda i, j: (layer, 0, 0))],
        out_specs=[pl.BlockSpec((tm, tn), lambda i, j: (i, j)),
                   pl.BlockSpec((tm, n2), lambda i, j: (i, 0))],
        out_shape=[jax.ShapeDtypeStruct((t, n), BF16), jax.ShapeDtypeStruct((t, n2), F32)],
        scratch_shapes=[pltpu.VMEM((tm, d), BF16)],
        compiler_params=_cparams("arbitrary", "arbitrary"),
        name="norm_gemm",
    )(x, norm_w, modr, modr, w, w2)


def _ffn_up_kernel(x_ref, nw_ref, sh_ref, sc_ref, wg_ref, wu_ref, o_ref, h_scr):
    @pl.when(pl.program_id(1) == 0)
    def _():
        _modulate_to(h_scr, x_ref, nw_ref, sh_ref, sc_ref)

    h = h_scr[...]
    gate = jnp.dot(h, wg_ref[...], preferred_element_type=F32)
    up = jnp.dot(h, wu_ref[...], preferred_element_type=F32)
    o_ref[...] = (gate * _sigmoid(gate) * up).astype(o_ref.dtype)


def ffn_up(x, norm_w, modr, w_gu, layer, nb, seq, tm, tn):
    t, d = x.shape
    f = w_gu.shape[-1] // 2
    tm = min(tm, seq)
    tn = min(tn, f)
    tps = seq // tm
    nf = f // tn
    return pl.pallas_call(
        _ffn_up_kernel,
        grid=(t // tm, nf),
        in_specs=[pl.BlockSpec((tm, d), lambda i, j: (i, 0)),
                  pl.BlockSpec((None, 1, d), lambda i, j: (layer, 0, 0)),
                  _mod_spec(d, layer, nb, 3, tps),
                  _mod_spec(d, layer, nb, 4, tps),
                  pl.BlockSpec((None, d, tn), lambda i, j: (layer, 0, j)),
                  pl.BlockSpec((None, d, tn), lambda i, j: (layer, 0, j + nf))],
        out_specs=pl.BlockSpec((tm, tn), lambda i, j: (i, j)),
        out_shape=jax.ShapeDtypeStruct((t, f), BF16),
        scratch_shapes=[pltpu.VMEM((tm, d), BF16)],
        compiler_params=_cparams("arbitrary", "arbitrary"),
        name="ffn_up",
    )(x, norm_w, modr, modr, w_gu, w_gu)


def _resid_gemm_kernel(a_ref, w_ref, x_ref, gt_ref, o_ref):
    y = jnp.dot(a_ref[...], w_ref[...], preferred_element_type=F32)
    o_ref[...] = x_ref[...] + gt_ref[...] * y


def resid_gemm(a, w, x, modr, layer, which_gt, nb, seq, tm, tn):
    t, k = a.shape
    d = x.shape[-1]
    tm = min(tm, seq)
    tn = min(tn, d)
    tps = seq // tm
    return pl.pallas_call(
        _resid_gemm_kernel,
        grid=(t // tm, d // tn),
        in_specs=[pl.BlockSpec((tm, k), lambda i, j: (i, 0)),
                  pl.BlockSpec((None, k, tn), lambda i, j: (layer, 0, j)),
                  pl.BlockSpec((tm, tn), lambda i, j: (i, j)),
                  pl.BlockSpec((None, 1, tn),
                               lambda i, j: ((layer * nb + i // tps) * 6 + which_gt, 0, j))],
        out_specs=pl.BlockSpec((tm, tn), lambda i, j: (i, j)),
        out_shape=jax.ShapeDtypeStruct((t, d), F32),
        compiler_params=_cparams("arbitrary", "arbitrary"),
        name="resid_gemm",
    )(a, w, x, modr)


def _merge_kernel(oa_ref, ob_ref, oc_ref, wa_ref, wb_ref, wc_ref, ga_ref, gb_ref, gc_ref, o_ref):
    def branch(o_r, w_r, g_r):
        y = jnp.dot(o_r[...], w_r[...], preferred_element_type=F32)
        return _sigmoid(g_r[...].astype(F32)) * y

    acc = branch(oa_ref, wa_ref, ga_ref) + branch(ob_ref, wb_ref, gb_ref) + branch(oc_ref, wc_ref, gc_ref)
    o_ref[...] = acc.astype(o_ref.dtype)


def merge_gemm(o_a, o_b, o_c, w_branch, p1, layer, tm, tn):
    t, kb = o_a.shape
    d = w_branch.shape[-1]
    tm = min(tm, t)
    tn = min(tn, d)
    gblk = GATE_OFF // tn
    dblk = d // tn

    def ospec():
        return pl.BlockSpec((tm, kb), lambda i, j: (i, 0))

    def wspec(r):
        return pl.BlockSpec((None, kb, tn), lambda i, j: (layer, r, j))

    def gspec(r):
        return pl.BlockSpec((tm, tn), lambda i, j: (i, gblk + r * dblk + j))

    return pl.pallas_call(
        _merge_kernel,
        grid=(t // tm, d // tn),
        in_specs=[ospec(), ospec(), ospec(), wspec(0), wspec(1), wspec(2), gspec(0), gspec(1), gspec(2)],
        out_specs=pl.BlockSpec((tm, tn), lambda i, j: (i, j)),
        out_shape=jax.ShapeDtypeStruct((t, d), BF16),
        compiler_params=_cparams("arbitrary", "arbitrary"),
        name="merge_gemm",
    )(o_a, o_b, o_c, w_branch, w_branch, w_branch, p1, p1, p1)


def _final_norm_kernel(x_ref, w_ref, o_ref):
    x = x_ref[...]
    ms = jnp.mean(x * x, axis=-1, keepdims=True)
    o_ref[...] = x * lax.rsqrt(ms + NORM_EPS) * w_ref[...]


def final_norm(x, w, tm):
    t, d = x.shape
    tm = min(tm, t)
    return pl.pallas_call(
        _final_norm_kernel,
        grid=(t // tm,),
        in_specs=[pl.BlockSpec((tm, d), lambda i: (i, 0)), pl.BlockSpec((1, d), lambda i: (0, 0))],
        out_specs=pl.BlockSpec((tm, d), lambda i: (i, 0)),
        out_shape=jax.ShapeDtypeStruct((t, d), F32),
        compiler_params=_cparams("arbitrary"),
        name="final_norm",
    )(x, w[None, :])


def _rms_rows(x, w):
    ms = jnp.mean(x * x, axis=-1, keepdims=True)
    return x * lax.rsqrt(ms + NORM_EPS) * w


def _mla_proj_kernel(cq_ref, kr_ref, ckv_ref, cs_ref, qnw_ref, kvnw_ref, wq_ref, wkv_ref,
                     q_ref, k_ref, vt_ref):
    scale =(MLA_NOPE + MLA_ROPE) ** -0.5 * math.log2(math.e)
    cc = cs_ref[:, 0:LANES]
    ss = cs_ref[:, LANES:2 * LANES]
    nq = _rms_rows(cq_ref[...].astype(F32), qnw_ref[...]).astype(BF16)
    q3 = jnp.dot(nq, wq_ref[...], preferred_element_type=F32)
    for h in range(MLA_HEADS):
        b0 = h * 3 * LANES
        nope = q3[:, b0:b0 + LANES]
        rope = q3[:, b0 + LANES:b0 + 2 * LANES] * cc + q3[:, b0 + 2 * LANES:b0 + 3 * LANES] * ss
        q_ref[:, 2 * h * LANES:(2 * h + 1) * LANES] = (nope * scale).astype(BF16)
        q_ref[:, (2 * h + 1) * LANES:(2 * h + 2) * LANES] = (rope * scale).astype(BF16)
    nkv = _rms_rows(ckv_ref[...].astype(F32), kvnw_ref[...]).astype(BF16)
    kv = jnp.dot(nkv, wkv_ref[...], preferred_element_type=F32)
    kr = kr_ref[...].astype(F32)
    krope = (kr[:, 0:LANES] * cc + kr[:, LANES:2 * LANES] * ss).astype(BF16)
    for h in range(MLA_HEADS):
        k_ref[:, 2 * h * LANES:(2 * h + 1) * LANES] = kv[:, h * LANES:(h + 1) * LANES].astype(BF16)
        k_ref[:, (2 * h + 1) * LANES:(2 * h + 2) * LANES] = krope
    ones = jnp.ones((VT_PAD, kv.shape[0]), BF16)
    for h in range(MLA_HEADS):
        vh = kv[:, (MLA_HEADS + h) * LANES:(MLA_HEADS + h + 1) * LANES]
        vt_ref[h, 0:MLA_V, :] = vh.T.astype(BF16)
        vt_ref[h, MLA_V:MLA_V + VT_PAD, :] = ones


def mla_proj(p1, cs, q_norm_w, kv_norm_w, wq3, wkv, layer, nb, seq, tm):
    t = p1.shape[0]
    tm = min(tm, seq)
    tps = seq // tm
    hq = MLA_HEADS * 2 * LANES
    return pl.pallas_call(
        _mla_proj_kernel,
        grid=(t // tm,),
        in_specs=[pl.BlockSpec((tm, MLA_Q_RANK), lambda i: (i, CQ_OFF // MLA_Q_RANK)),
                  pl.BlockSpec((tm, 2 * LANES), lambda i: (i, KR_OFF // (2 * LANES))),
                  pl.BlockSpec((tm, MLA_KV_RANK), lambda i: (i, CKV_OFF // MLA_KV_RANK)),
                  pl.BlockSpec((tm, 2 * LANES), lambda i: (i, 0)),
                  pl.BlockSpec((None, 1, MLA_Q_RANK), lambda i: (layer, 0, 0)),
                  pl.BlockSpec((None, 1, MLA_KV_RANK), lambda i: (layer, 0, 0)),
                  pl.BlockSpec((None, MLA_Q_RANK, 3 * MLA_HEADS * LANES), lambda i: (layer, 0, 0)),
                  pl.BlockSpec((None, MLA_KV_RANK, 2 * MLA_HEADS * LANES), lambda i: (layer, 0, 0))],
        out_specs=[pl.BlockSpec((tm, hq), lambda i: (i, 0)),
                   pl.BlockSpec((tm, hq), lambda i: (i, 0)),
                   pl.BlockSpec((None, MLA_HEADS, MLA_V + VT_PAD, tm), lambda i: (i // tps, 0, 0, i % tps))],
        out_shape=[jax.ShapeDtypeStruct((t, hq), BF16),
                   jax.ShapeDtypeStruct((t, hq), BF16),
                   jax.ShapeDtypeStruct((nb, MLA_HEADS, MLA_V + VT_PAD, seq), BF16)],
        compiler_params=_cparams("arbitrary"),
        name="mla_proj",
    )(p1, p1, p1, cs, q_norm_w, kv_norm_w, wq3, wkv)


def _flash_kernel(it_ref, jt_ref, q_ref, k_ref, vt_ref, o_ref, m_scr, acc_scr):
    t = pl.program_id(2)
    i = it_ref[t]
    j = jt_ref[t]
    heads = range(FLASH_HEADS)

    @pl.when(j == 0)
    def _():
        m_scr[...] = jnp.full(m_scr.shape, -jnp.inf, F32)
        acc_scr[...] = jnp.zeros(acc_scr.shape, F32)

    def step(masked):
        sts = [lax.dot_general(k_ref[:, 2 * g * LANES:2 * (g + 1) * LANES],
                               q_ref[:, 2 * g * LANES:2 * (g + 1) * LANES],
                               (((1,), (1,)), ((), ())), preferred_element_type=F32) for g in heads]
        if masked:
            keep = _iota(sts[0].shape, 1) >= _iota(sts[0].shape, 0)
            sts = [jnp.where(keep, s, -jnp.inf) for s in sts]
        m_prev = [m_scr[g] for g in heads]
        m_new = [jnp.maximum(mp, jnp.max(s, axis=0, keepdims=True)) for mp, s in zip(m_prev, sts)]
        pts = [jnp.exp2(s - mn).astype(BF16) for s, mn in zip(sts, m_new)]
        alphas = [jnp.exp2(mp - mn) for mp, mn in zip(m_prev, m_new)]
        pvs = [jnp.dot(vt_ref[g], pt, preferred_element_type=F32) for g, pt in zip(heads, pts)]
        for g in heads:
            acc_scr[g] = alphas[g] * acc_scr[g] + pvs[g]
            m_scr[g] = m_new[g]

    @pl.when(j < i)
    def _():
        step(False)

    @pl.when(j == i)
    def _():
        step(True)
        for g in heads:
            acc = acc_scr[g]
            o_t = acc[0:MLA_V, :] / acc[MLA_V:MLA_V + 1, :]
            o_ref[:, g * LANES:(g + 1) * LANES] = o_t.T.astype(o_ref.dtype)


def flash_attn(q, k, vt, nb, seq, tq):
    tq = min(tq, seq)
    nq = seq // tq
    hg = FLASH_HEADS
    pairs = [(i, j) for i in range(nq) for j in range(i + 1)]
    it = jnp.asarray([p[0] for p in pairs], jnp.int32)
    jt = jnp.asarray([p[1] for p in pairs], jnp.int32)
    grid_spec = pltpu.PrefetchScalarGridSpec(
        num_scalar_prefetch=2,
        grid=(nb, MLA_HEADS // hg, len(pairs)),
        in_specs=[pl.BlockSpec((tq, hg * 2 * LANES), lambda b, h, t, it_r, jt_r: (b * nq + it_r[t], h)),
                  pl.BlockSpec((tq, hg * 2 * LANES), lambda b, h, t, it_r, jt_r: (b * nq + jt_r[t], h)),
                  pl.BlockSpec((None, hg, MLA_V + VT_PAD, tq), lambda b, h, t, it_r, jt_r: (b, h, 0, jt_r[t]))],
        out_specs=pl.BlockSpec((tq, hg * LANES), lambda b, h, t, it_r, jt_r: (b * nq + it_r[t], h)),
        scratch_shapes=[pltpu.VMEM((hg, 1, tq), F32), pltpu.VMEM((hg, MLA_V + VT_PAD, tq), F32)],
    )
    return pl.pallas_call(
        _flash_kernel,
        grid_spec=grid_spec,
        out_shape=jax.ShapeDtypeStruct((nb * seq, MLA_HEADS * LANES), BF16),
        compiler_params=_cparams("arbitrary", "arbitrary", "arbitrary"),
        name="flash_attn",
    )(it, jt, q, k, vt)


def _shifted_rows(x, halo, d):
    xr = pltpu.roll(x, d, 0)
    hr = pltpu.roll(halo, d, 0)
    top = jnp.where(_iota(halo.shape, 0) < d, hr, xr[0:SUBLANES])
    return jnp.concatenate([top, xr[SUBLANES:]], axis=0)


def _gdn_prep_kernel(q_ref, k_ref, v_ref, qh_ref, kh_ref, vh_ref, cwq_ref, cwk_ref, cwv_ref,
                     p2_ref, alog_ref, dtb_ref, qo_ref, ko_ref, vo_ref, bg_ref, *, tiles_per_seq):
    first = (pl.program_id(0) % tiles_per_seq) == 0

    def conv_silu(x_ref, h_ref, w_ref):
        x = x_ref[...].astype(F32)
        halo = jnp.where(first, 0.0, h_ref[...].astype(F32))
        w = w_ref[...]
        acc = x * w[GDN_CONV - 1:GDN_CONV, :]
        for d in range(1, GDN_CONV):
            acc = acc + _shifted_rows(x, halo, d) * w[GDN_CONV - 1 - d:GDN_CONV - d, :]
        return acc * _sigmoid(acc)

    def l2n(x, mult):
        outs = []
        for h in range(GDN_HEADS):
            xh = x[:, h * LANES:(h + 1) * LANES]
            ss = jnp.sum(xh * xh, axis=-1, keepdims=True)
            outs.append(xh * (lax.rsqrt(ss + 1e-6) * mult))
        return jnp.concatenate(outs, axis=1)

    qo_ref[...] = l2n(conv_silu(q_ref, qh_ref, cwq_ref), GDN_DK ** -0.5).astype(BF16)
    ko_ref[...] = l2n(conv_silu(k_ref, kh_ref, cwk_ref), 1.0).astype(BF16)
    vo_ref[...] = conv_silu(v_ref, vh_ref, cwv_ref).astype(BF16)
    p2 = p2_ref[...]
    lane = _iota(p2.shape, 1)
    g = -jnp.exp(alog_ref[...]) * _softplus(p2 + dtb_ref[...])
    bg_ref[...] = jnp.where(lane < GDN_HEADS, _sigmoid(p2), g)


def gdn_prep(p1, p2, conv_w, a_log, dt_bias, layer, seq, tm):
    t = p1.shape[0]
    tm = min(tm, seq)
    w = GDN_HEADS * GDN_DK
    cblk = GQ_OFF // w
    hb = tm // SUBLANES

    def xspec(r):
        return pl.BlockSpec((tm, w), lambda i: (i, cblk + r))

    def hspec(r):
        return pl.BlockSpec((SUBLANES, w), lambda i: (jnp.maximum(i * hb - 1, 0), cblk + r))

    def cwspec(r):
        return pl.BlockSpec((None, GDN_CONV, w), lambda i: (layer, 0, r))

    pad = jnp.zeros((LANES - 2 * GDN_HEADS,), F32)
    z8 = jnp.zeros((GDN_HEADS,), F32)
    alog_row = jnp.concatenate([z8, a_log[layer], pad])[None, :]
    dtb_row = jnp.concatenate([z8, dt_bias[layer], pad])[None, :]
    return pl.pallas_call(
        functools.partial(_gdn_prep_kernel, tiles_per_seq=seq // tm),
        grid=(t // tm,),
        in_specs=[xspec(0), xspec(1), xspec(2), hspec(0), hspec(1), hspec(2),
                  cwspec(0), cwspec(1), cwspec(2),
                  pl.BlockSpec((tm, LANES), lambda i: (i, 2)),
                  pl.BlockSpec((1, LANES), lambda i: (0, 0)),
                  pl.BlockSpec((1, LANES), lambda i: (0, 0))],
        out_specs=[pl.BlockSpec((tm, w), lambda i: (i, 0))] * 3 + [pl.BlockSpec((tm, LANES), lambda i: (i, 0))],
        out_shape=[jax.ShapeDtypeStruct((t, w), BF16)] * 3 + [jax.ShapeDtypeStruct((t, LANES), F32)],
        compiler_params=_cparams("arbitrary"),
        name="gdn_prep",
    )(p1, p1, p1, p1, p1, p1, conv_w, conv_w, conv_w, p2, alog_row, dtb_row)


def _gdn_chunk_kernel(q_ref, k_ref, v_ref, z_ref, bg_ref, nw_ref, o_ref, st_ref, uw_scr, qk_scr, qg_scr, kg_scr):
    nb, tb, _ = q_ref.shape
    c = CHUNK
    nck = tb // c

    @pl.when(pl.program_id(0) == 0)
    def _():
        st_ref[...] = jnp.zeros(st_ref.shape, F32)

    r = _iota((tb, tb), 0)
    cc = _iota((tb, tb), 1)
    same = (r // c) == (cc // c)
    incl = same & (r >= cc)
    strict = same & (r > cc)
    units = [(b, h) for b in range(nb) for h in range(GDN_HEADS)]
    lmat = jnp.where(incl, 1.0, 0.0).astype(BF16)
    g_cum = [_dot_exact_lhs(lmat, bg_ref[b], 3) for b in range(nb)]
    g_tot = [jnp.concatenate([jnp.broadcast_to(g[(ci + 1) * c - 1:(ci + 1) * c, :], (c, LANES))
                              for ci in range(nck)], axis=0) for g in g_cum]
    g_cum_t = [g.T for g in g_cum]

    def cols(h):
        return slice(h * LANES, (h + 1) * LANES)

    def glane(h):
        return slice(GDN_HEADS + h, GDN_HEADS + h + 1)

    for g0 in range(0, len(units), GROUP):
        grp = units[g0:g0 + GROUP]
        ks = [k_ref[b, :, cols(h)] for b, h in grp]
        gcs = [g_cum[b][:, glane(h)] for b, h in grp]
        betas = [bg_ref[b, :, h:h + 1] for b, h in grp]
        decs = [jnp.exp(jnp.where(incl, gc - g_cum_t[b][glane(h), :], -jnp.inf))
                for gc, (b, h) in zip(gcs, grp)]
        qs = [q_ref[b, :, cols(h)] for b, h in grp]
        kqs = [_dot_nt(jnp.concatenate([k, q], axis=0), k) for k, q in zip(ks, qs)]
        a_s = [jnp.where(strict, kq[:tb] * dec, 0.0) * beta for kq, dec, beta in zip(kqs, decs, betas)]
        ns = _tri_inv_m1(a_s, c)
        for idx, (b, h) in enumerate(grp):
            u = g0 + idx
            q = qs[idx]
            kf = ks[idx].astype(F32)
            vf = v_ref[b, :, cols(h)].astype(F32)
            eg = jnp.exp(gcs[idx])
            rhs = jnp.concatenate([betas[idx] * vf, (betas[idx] * eg) * kf], axis=1)
            uw_scr[u] = rhs + _dot(ns[idx], rhs)
            qk_scr[u] = (kqs[idx][tb:] * decs[idx]).astype(BF16)
            qg_scr[u] = (q.astype(F32) * eg).astype(BF16)
            kg_scr[u] = (kf * jnp.exp(g_tot[b][:, glane(h)] - gcs[idx])).astype(BF16)

    states = [st_ref[u] for u in range(len(units))]
    w_parts = [[] for _ in units]
    og_parts = [[] for _ in units]
    for ci in range(nck):
        rows = slice(ci * c, (ci + 1) * c)
        sbs = [s.astype(BF16) for s in states]
        wqs = [_dot(jnp.concatenate([uw_scr[u, rows, GDN_DV:].astype(BF16), qg_scr[u, rows, :]], axis=0), sbs[u])
               for u in range(len(units))]
        for u in range(len(units)):
            w_parts[u].append((uw_scr[u, rows, :GDN_DV] - wqs[u][:c]).astype(BF16))
            og_parts[u].append(wqs[u][c:])
        states = [jnp.exp(g_tot[b][ci * c:ci * c + 1, glane(h)]) * states[u]
                  + _dot_tn(kg_scr[u, rows, :], w_parts[u][ci])
                  for u, (b, h) in enumerate(units)]

    nw = nw_ref[...]
    for u, (b, h) in enumerate(units):
        o = jnp.concatenate(og_parts[u], axis=0) + _dot(qk_scr[u], jnp.concatenate(w_parts[u], axis=0))
        zf = z_ref[b, :, cols(h)].astype(F32)
        o_ref[b, :, cols(h)] = (_rms_rows(o, nw) * (zf * _sigmoid(zf))).astype(o_ref.dtype)
        st_ref[u] = states[u]


def gdn_chunk(qn, kn, vc, p1, bg, norm_w, layer, nb, seq):
    t, w = qn.shape
    tb = min(GDN_STEP, seq)
    zblk = (GQ_OFF + 3 * w) // w
    nu = nb * GDN_HEADS

    def xspec(cb):
        return pl.BlockSpec((nb, tb, w), lambda s: (0, s, cb))

    out = pl.pallas_call(
        _gdn_chunk_kernel,
        grid=(seq // tb,),
        in_specs=[xspec(0), xspec(0), xspec(0), xspec(zblk),
                  pl.BlockSpec((nb, tb, LANES), lambda s: (0, s, 0)),
                  pl.BlockSpec((None, 1, GDN_DV), lambda s: (layer, 0, 0))],
        out_specs=xspec(0),
        out_shape=jax.ShapeDtypeStruct((nb, seq, w), BF16),
        scratch_shapes=[pltpu.VMEM((nu, GDN_DK, GDN_DV), F32),
                        pltpu.VMEM((nu, tb, 2 * GDN_DV), F32),
                        pltpu.VMEM((nu, tb, tb), BF16),
                        pltpu.VMEM((nu, tb, GDN_DK), BF16),
                        pltpu.VMEM((nu, tb, GDN_DK), BF16)],
        compiler_params=_cparams("arbitrary"),
        name="gdn_chunk",
    )(qn.reshape(nb, seq, w), kn.reshape(nb, seq, w), vc.reshape(nb, seq, w),
      p1.reshape(nb, seq, p1.shape[-1]), bg.reshape(nb, seq, LANES), norm_w)
    return out.reshape(t, w)


def _head_sum_matrix():
    return ((_iota((LANES, LANES), 0) // RWKV_HEAD) == (_iota((LANES, LANES), 1) // RWKV_HEAD)).astype(BF16)


def _rwkv_prep_kernel(r_ref, k_ref, v_ref, x_ref, rh_ref, kh_ref, vh_ref, xh_ref,
                      mur_ref, muk_ref, muv_ref, mux_ref, w0_ref, wup_ref, a0_ref, aup_ref, gup_ref,
                      kk_w_ref, ka_ref, rk_ref,
                      kkd_ref, rt_ref, kt_ref, bt_ref, kh_o_ref, bh_o_ref, v_o_ref, pc_ref, bonus_ref, g_ref,
                      *, tiles_per_seq):
    first = (pl.program_id(0) % tiles_per_seq) == 0
    c = CHUNK
    tm = r_ref.shape[0]

    def shift_mix(x_r, h_r, mu_r):
        x = x_r[...].astype(F32)
        halo = jnp.where(first, 0.0, h_r[...].astype(F32))
        prev = _shifted_rows(x, halo, 1)
        return x + (prev - x) * mu_r[...]

    r = shift_mix(r_ref, rh_ref, mur_ref)
    k = shift_mix(k_ref, kh_ref, muk_ref)
    v = shift_mix(v_ref, vh_ref, muv_ref)
    xs = shift_mix(x_ref, xh_ref, mux_ref)
    xw = xs[:, 0:LANES]
    lane = _iota(xw.shape, 1)
    tw = jnp.where(lane < RWKV_W_RANK, jnp.tanh(xw), 0.0)
    xa = jnp.where(lane >= RWKV_W_RANK, xw, 0.0)
    sg = _sigmoid(xs[:, LANES:2 * LANES])
    w_log = -_softplus(-(w0_ref[...] + _dot(tw, wup_ref[...]))) - 0.5
    lw = -jnp.exp(w_log)
    a = _sigmoid(a0_ref[...] + _dot(xa, aup_ref[...]))
    g = _dot(sg, gup_ref[...])
    hs = _head_sum_matrix()
    kkw = k * kk_w_ref[...]
    k2 = k * (1.0 + (a - 1.0) * ka_ref[...])
    rkr = r * k2 * rk_ref[...]
    kk_parts = []
    bonus_parts = []
    for blk in range(RWKV_WIDTH // LANES):
        cols = slice(blk * LANES, (blk + 1) * LANES)
        x_blk = kkw[:, cols]
        ss = _dot_exact_rhs(x_blk * x_blk, hs, 2)
        kk_parts.append(x_blk * lax.rsqrt(ss + 1e-6))
        bonus_parts.append(_dot_exact_rhs(rkr[:, cols], hs, 2) * v[:, cols])
    kk = jnp.concatenate(kk_parts, axis=1)
    bonus_ref[...] = jnp.concatenate(bonus_parts, axis=1).astype(bonus_ref.dtype)
    g_ref[...] = g.astype(g_ref.dtype)
    b = kk * a
    ri = _iota((tm, tm), 0)
    cj = _iota((tm, tm), 1)
    same = (ri // c) == (cj // c)
    gi = _dot_exact_lhs(jnp.where(same & (ri >= cj), 1.0, 0.0).astype(BF16), lw, 3)
    ge = gi - lw
    e_neg = jnp.exp(-gi)
    kkd_ref[...] = (kk * jnp.exp(ge)).astype(BF16)
    rt_ref[...] = (r * jnp.exp(gi)).astype(BF16)
    kt_ref[...] = (k2 * e_neg).astype(BF16)
    bt_ref[...] = (b * e_neg).astype(BF16)
    v_o_ref[...] = v.astype(BF16)
    for ci in range(tm // c):
        rows = slice(ci * c, (ci + 1) * c)
        egl = jnp.exp(gi[(ci + 1) * c - 1:(ci + 1) * c, :])
        pc_ref[ci] = egl
        e_tail = egl * e_neg[rows]
        kh_o_ref[rows, :] = (k2[rows] * e_tail).astype(BF16)
        bh_o_ref[rows, :] = (b[rows] * e_tail).astype(BF16)


def rwkv_prep(p1, p2, mu, w0, w_up, a0, a_up, g_up, k_k, k_a, r_k, layer, seq, tm):
    t = p1.shape[0]
    tm = min(tm, seq)
    w = RWKV_WIDTH
    cblk = RR_OFF // w
    hb = tm // SUBLANES
    xw2 = 2 * LANES

    def xspec(r):
        return pl.BlockSpec((tm, w), lambda i: (i, cblk + r))

    def hspec(r):
        return pl.BlockSpec((SUBLANES, w), lambda i: (jnp.maximum(i * hb - 1, 0), cblk + r))

    def row(width, blk):
        return pl.BlockSpec((None, 1, width), lambda i: (layer, 0, blk))

    zpad = jnp.zeros((w_up.shape[0], RWKV_W_RANK, w), F32)
    wup_p = jnp.concatenate([w_up, zpad], axis=1).astype(BF16)
    aup_p = jnp.concatenate([zpad, a_up], axis=1).astype(BF16)
    gup_b = g_up.astype(BF16)
    mu3 = mu[:, None, :]
    rk_row = r_k.reshape(r_k.shape[0], 1, w)
    outs = [jax.ShapeDtypeStruct((t, w), BF16)] * 7
    outs += [jax.ShapeDtypeStruct((t // CHUNK, 1, w), F32),
             jax.ShapeDtypeStruct((t, w), BF16), jax.ShapeDtypeStruct((t, w), BF16)]
    ospec = pl.BlockSpec((tm, w), lambda i: (i, 0))
    out_specs = [ospec] * 7 + [pl.BlockSpec((tm // CHUNK, 1, w), lambda i: (i, 0, 0)), ospec, ospec]
    return pl.pallas_call(
        functools.partial(_rwkv_prep_kernel, tiles_per_seq=seq // tm),
        grid=(t // tm,),
        in_specs=[xspec(0), xspec(1), xspec(2), pl.BlockSpec((tm, xw2), lambda i: (i, 0)),
                  hspec(0), hspec(1), hspec(2),
                  pl.BlockSpec((SUBLANES, xw2), lambda i: (jnp.maximum(i * hb - 1, 0), 0)),
                  row(w, 0), row(w, 1), row(w, 2),
                  pl.BlockSpec((None, 1, xw2), lambda i: (layer, 0, 3 * w // xw2)),
                  row(w, 0),
                  pl.BlockSpec((None, LANES, w), lambda i: (layer, 0, 0)),
                  row(w, 0),
                  pl.BlockSpec((None, LANES, w), lambda i: (layer, 0, 0)),
                  pl.BlockSpec((None, RWKV_G_RANK, w), lambda i: (layer, 0, 0)),
                  row(w, 0), row(w, 0), row(w, 0)],
        out_specs=out_specs,
        out_shape=outs,
        compiler_params=_cparams("arbitrary"),
        name="rwkv_prep",
    )(p1, p1, p1, p2, p1, p1, p1, p2, mu3, mu3, mu3, mu3, w0[:, None, :], wup_p, a0[:, None, :], aup_p,
      gup_b, k_k[:, None, :], k_a[:, None, :], rk_row)


def _rwkv_chunk_kernel(kkd_ref, rt_ref, kt_ref, bt_ref, kh_ref, bh_ref, v_ref, pc_ref, y_ref, st_ref,
                       n_scr, arb_scr, uloc_scr, yloc_scr):
    nb, tb, w = kkd_ref.shape
    c = CHUNK
    nck = tb // c
    n2 = 2 * tb
    npair = w // LANES

    @pl.when(pl.program_id(0) == 0)
    def _():
        st_ref[...] = jnp.zeros(st_ref.shape, F32)

    r = _iota((n2, n2), 0)
    cc = _iota((n2, n2), 1)
    same = (r // c) == (cc // c)
    incl = same & (r >= cc)
    strict = same & (r > cc)
    head0 = _iota((1, LANES), 1) < RWKV_HEAD
    bd = (_iota((LANES, LANES), 0) // RWKV_HEAD) == (_iota((LANES, LANES), 1) // RWKV_HEAD)
    units = [(b, p) for b in range(nb) for p in range(npair)]

    def cols(p):
        return slice(p * LANES, (p + 1) * LANES)

    def rows(ci):
        return slice(ci * c, (ci + 1) * c)

    def blk(ci):
        return slice(ci * LANES, (ci + 1) * LANES)

    def stack_heads(x):
        x0 = jnp.where(head0, x, jnp.zeros_like(x))
        x1 = jnp.where(head0, jnp.zeros_like(x), x)
        return jnp.concatenate([piece[rows(ci)] for ci in range(nck) for piece in (x0, x1)], axis=0)

    def stack_dup(x):
        return jnp.concatenate([x[rows(ci)] for ci in range(nck) for _ in range(2)], axis=0)

    for g0 in range(0, len(units), GROUP):
        grp = units[g0:g0 + GROUP]
        lks = [stack_heads(kkd_ref[b, :, cols(p)]) for b, p in grp]
        lrs = [stack_heads(rt_ref[b, :, cols(p)]) for b, p in grp]
        rks = [stack_dup(kt_ref[b, :, cols(p)]) for b, p in grp]
        rbs = [stack_dup(bt_ref[b, :, cols(p)]) for b, p in grp]
        v2s = [stack_dup(v_ref[b, :, cols(p)]) for b, p in grp]
        pws = [[_dot_nt(jnp.concatenate([lk[blk(ci)], lr[blk(ci)]], axis=0),
                        jnp.concatenate([rk[blk(ci)], rb[blk(ci)]], axis=0)) for ci in range(nck)]
               for lk, lr, rk, rb in zip(lks, lrs, rks, rbs)]
        zblk = jnp.zeros((LANES, LANES), F32)

        def chunk_diag(pw, rsl, csl):
            return jnp.concatenate(
                [jnp.concatenate([pw[ci][rsl, csl] if cj == ci else zblk for cj in range(nck)], axis=1)
                 for ci in range(nck)], axis=0)

        lo, hi = slice(0, LANES), slice(LANES, 2 * LANES)
        a_kb = [jnp.where(strict, chunk_diag(pw, lo, hi), 0.0) for pw in pws]
        a_kk = [jnp.where(strict, chunk_diag(pw, lo, lo), 0.0) for pw in pws]
        ns = _tri_inv_m1(a_kb, c)
        avs = [_dot(a, v2) for a, v2 in zip(a_kk, v2s)]
        for idx in range(len(grp)):
            u = g0 + idx
            nbf = ns[idx].astype(BF16)
            n_scr[u] = nbf
            uloc_scr[u] = avs[idx] + _dot(nbf, avs[idx])
            a_rk = jnp.where(incl, chunk_diag(pws[idx], hi, lo), 0.0)
            yloc_scr[u] = _dot(a_rk, v2s[idx])
            arb_scr[u] = jnp.where(incl, chunk_diag(pws[idx], hi, hi), 0.0).astype(BF16)

    nu = len(units)
    states = [st_ref[u] for u in range(nu)]
    rs_parts = [[] for _ in units]
    ub_parts = [[] for _ in units]
    for ci in range(nck):
        sbs = [s.astype(BF16) for s in states]
        kss = [_dot_nt(jnp.concatenate([kkd_ref[b, rows(ci), cols(p)], rt_ref[b, rows(ci), cols(p)]], axis=0), sbs[u])
               for u, (b, p) in enumerate(units)]
        ks2 = [jnp.concatenate([ks[:c], ks[:c]], axis=0) for ks in kss]
        tks = [ks2[u] + _dot(n_scr[u, blk(ci), blk(ci)], ks2[u]) for u in range(nu)]
        for u in range(nu):
            u2 = uloc_scr[u, blk(ci), :] + tks[u]
            ub_parts[u].append(jnp.where(head0, u2[:c], u2[c:]).astype(BF16))
            rs_parts[u].append(kss[u][c:])
        states = [states[u] * pc_ref[b, ci, :, cols(p)]
                  + jnp.where(bd, _dot_tn(jnp.concatenate([v_ref[b, rows(ci), cols(p)], ub_parts[u][ci]], axis=0),
                                          jnp.concatenate([kh_ref[b, rows(ci), cols(p)],
                                                           -bh_ref[b, rows(ci), cols(p)]], axis=0)), 0.0)
                  for u, (b, p) in enumerate(units)]

    for u, (b, p) in enumerate(units):
        u2all = jnp.concatenate([ub_parts[u][ci] for ci in range(nck) for _ in range(2)], axis=0)
        yall = yloc_scr[u] - _dot(arb_scr[u], u2all)
        for ci in range(nck):
            yb = yall[blk(ci), :]
            y_ref[b, rows(ci), cols(p)] = rs_parts[u][ci] + jnp.where(head0, yb[:c], yb[c:])
        st_ref[u] = states[u]


def rwkv_chunk(tensors, pc, nb, seq):
    t, w = tensors[0].shape
    tb = min(RWKV_STEP, seq)
    nck = tb // CHUNK
    nu = nb * (w // LANES)
    xspec = pl.BlockSpec((nb, tb, w), lambda s: (0, s, 0))
    out = pl.pallas_call(
        _rwkv_chunk_kernel,
        grid=(seq // tb,),
        in_specs=[xspec] * 7 + [pl.BlockSpec((nb, nck, 1, w), lambda s: (0, s, 0, 0))],
        out_specs=xspec,
        out_shape=jax.ShapeDtypeStruct((nb, seq, w), F32),
        scratch_shapes=[pltpu.VMEM((nu, LANES, LANES), F32),
                        pltpu.VMEM((nu, 2 * tb, 2 * tb), BF16),
                        pltpu.VMEM((nu, 2 * tb, 2 * tb), BF16),
                        pltpu.VMEM((nu, 2 * tb, LANES), F32),
                        pltpu.VMEM((nu, 2 * tb, LANES), F32)],
        compiler_params=_cparams("arbitrary"),
        name="rwkv_chunk",
    )(*[x.reshape(nb, seq, w) for x in tensors], pc.reshape(nb, seq // CHUNK, 1, w))
    return out.reshape(t, w)


def _rwkv_post_kernel(y_ref, bonus_ref, g_ref, lw_ref, lb_ref, o_ref):
    hm = _head_sum_matrix() * (1.0 / RWKV_HEAD)
    hm = hm.astype(BF16)
    for blk in range(RWKV_WIDTH // LANES):
        cols = slice(blk * LANES, (blk + 1) * LANES)
        y = y_ref[:, cols]
        mean = _dot_exact_rhs(y, hm, 3)
        yc = y - mean
        var = _dot_exact_rhs(yc * yc, hm, 3)
        yn = yc * lax.rsqrt(var + RWKV_LN_EPS) * lw_ref[:, cols] + lb_ref[:, cols]
        out = (yn + bonus_ref[:, cols].astype(F32)) * g_ref[:, cols].astype(F32)
        o_ref[:, cols] = out.astype(o_ref.dtype)


def rwkv_post(y, bonus, g, lnx_w, lnx_b, layer, tm):
    t, w = y.shape
    tm = min(tm, t)
    xspec = pl.BlockSpec((tm, w), lambda i: (i, 0))
    rspec = pl.BlockSpec((None, 1, w), lambda i: (layer, 0, 0))
    return pl.pallas_call(
        _rwkv_post_kernel,
        grid=(t // tm,),
        in_specs=[xspec, xspec, xspec, rspec, rspec],
        out_specs=xspec,
        out_shape=jax.ShapeDtypeStruct((t, w), BF16),
        compiler_params=_cparams("arbitrary"),
        name="rwkv_post",
    )(y, bonus, g, lnx_w[:, None, :], lnx_b[:, None, :])


def _relayout_w_in(w_in):
    nl, d, _ = w_in.shape
    wt = jnp.swapaxes(w_in, 1, 2).astype(BF16)

    def rows(lo, n):
        return wt[:, lo:lo + n, :]

    def zeros(n):
        return jnp.zeros((nl, n, d), BF16)

    kr = rows(_O_KR, MLA_ROPE)
    half = MLA_ROPE // 2
    kr_swap = jnp.concatenate([-kr[:, half:, :], kr[:, :half, :]], axis=1)
    w1 = jnp.concatenate([
        rows(_O_CQ, MLA_Q_RANK),
        kr, zeros(LANES - MLA_ROPE), kr_swap, zeros(LANES - MLA_ROPE),
        rows(_O_CKV, MLA_KV_RANK), zeros(GQ_OFF - CKV_OFF - MLA_KV_RANK),
        rows(_O_GDN, 4096),
        rows(_O_RWKV, 3072),
        rows(_O_GATE, 3 * D_MODEL)], axis=1)
    w2 = jnp.concatenate([rows(_O_XW, 256), rows(_O_GB, 16), zeros(NP2 - 272)], axis=1)
    return w1, w2


def _relayout_w_uq(w_uq):
    nl, r, _ = w_uq.shape
    wq = w_uq.reshape(nl, r, MLA_HEADS, MLA_NOPE + MLA_ROPE)
    nope = wq[..., :MLA_NOPE]
    rope = wq[..., MLA_NOPE:]
    half = MLA_ROPE // 2
    swap = jnp.concatenate([-rope[..., half:], rope[..., :half]], axis=-1)
    z = jnp.zeros(rope.shape[:-1] + (LANES - MLA_ROPE,), w_uq.dtype)
    return jnp.concatenate([nope, rope, z, swap, z], axis=-1).reshape(nl, r, 3 * MLA_HEADS * LANES).astype(BF16)


def _relayout_w_ukv(w_ukv):
    nl, r, _ = w_ukv.shape
    wkv = w_ukv.reshape(nl, r, MLA_HEADS, MLA_NOPE + MLA_V)
    wk = wkv[..., :MLA_NOPE].reshape(nl, r, MLA_HEADS * MLA_NOPE)
    wv = wkv[..., MLA_NOPE:].reshape(nl, r, MLA_HEADS * MLA_V)
    return jnp.concatenate([wk, wv], axis=-1).astype(BF16)


def _rope_table(positions):
    inv = 1.0 / (ROPE_THETA ** (jnp.arange(0, MLA_ROPE, 2, dtype=F32) / MLA_ROPE))
    ang = positions.astype(F32)[..., None] * inv
    cos, sin = jnp.cos(ang), jnp.sin(ang)
    z = jnp.zeros(cos.shape[:-1] + (LANES - MLA_ROPE,), F32)
    cs = jnp.concatenate([cos, cos, z, sin, sin, z], axis=-1)
    return cs.reshape(-1, 2 * LANES)


def kernel(x, c, positions, w_ada, b_ada, norm1_w, w_in, mla_q_norm_w, mla_w_uq, mla_kv_norm_w, mla_w_ukv,
           gdn_conv_w, gdn_a_log, gdn_dt_bias, gdn_norm_w, rwkv_mu, rwkv_w0, rwkv_w_up, rwkv_a0, rwkv_a_up,
           rwkv_g_up, rwkv_k_k, rwkv_k_a, rwkv_r_k, rwkv_lnx_w, rwkv_lnx_b, w_branch, w_out, norm2_w,
           w_gate_up, w_down, final_norm_w):
    nb, seq, d = x.shape
    nl = w_in.shape[0]
    assert seq % (2 * CHUNK) == 0 and d == D_MODEL

    w1, w2 = _relayout_w_in(w_in)
    wq3 = _relayout_w_uq(mla_w_uq)
    wkv = _relayout_w_ukv(mla_w_ukv)
    w_branch_b = w_branch.astype(BF16)
    w_out_b = w_out.astype(BF16)
    w_gu_b = w_gate_up.astype(BF16)
    w_down_b = w_down.astype(BF16)
    cs = _rope_table(positions)

    mod = adaln_mod(c, w_ada, b_ada)
    modr = mod.reshape(nl * nb * 6, 1, d)
    n1 = norm1_w[:, None, :]
    n2 = norm2_w[:, None, :]
    qnw = mla_q_norm_w[:, None, :]
    kvnw = mla_kv_norm_w[:, None, :]
    gnw = gdn_norm_w[:, None, :]

    xf = x.reshape(nb * seq, d)
    for l in range(nl):
        p1, p2 = norm_gemm(xf, n1, modr, w1, w2, l, 0, nb, seq, *GEMM_TILES["w_in"])
        q, k, vt = mla_proj(p1, cs, qnw, kvnw, wq3, wkv, l, nb, seq, ROW_TILE)
        o_a = flash_attn(q, k, vt, nb, seq, ATTN_TILE)
        qn, kn, vc, bg = gdn_prep(p1, p2, gdn_conv_w, gdn_a_log, gdn_dt_bias, l, seq, ROW_TILE)
        o_b = gdn_chunk(qn, kn, vc, p1, bg, gnw, l, nb, seq)
        prep = rwkv_prep(p1, p2, rwkv_mu, rwkv_w0, rwkv_w_up, rwkv_a0, rwkv_a_up, rwkv_g_up,
                         rwkv_k_k, rwkv_k_a, rwkv_r_k, l, seq, ROW_TILE)
        y = rwkv_chunk(prep[:7], prep[7], nb, seq)
        o_c = rwkv_post(y, prep[8], prep[9], rwkv_lnx_w, rwkv_lnx_b, l, ROW_TILE)
        merged = merge_gemm(o_a, o_b, o_c, w_branch_b, p1, l, *GEMM_TILES["merge"])
        xf = resid_gemm(merged, w_out_b, xf, modr, l, 2, nb, seq, *GEMM_TILES["w_out"])
        act = ffn_up(xf, n2, modr, w_gu_b, l, nb, seq, *GEMM_TILES["ffn_up"])
        xf = resid_gemm(act, w_down_b, xf, modr, l, 5, nb, seq, *GEMM_TILES["ffn_down"])
    return final_norm(xf, final_norm_w, ROW_TILE).reshape(nb, seq, d)
```
